```python
import math
import jax, jax.numpy as jnp
from jax import lax
import numpy as np

D_MODEL = 1024
BATCH = 16
SEQ = 2048
DEPTH = 4

GRID_W = 64
CTX_LEN = 256
HEAD_DIM = 64
ATT_W = D_MODEL // 2
NA_HEADS = ATT_W // HEAD_DIM
NA_ROWS_MAX = 8
NA_COLS = 16
NA_QCOLS = 16
SG_W = D_MODEL // 4
SG_GROUPS = 4
SG_CHUNK = 128
CV_W = D_MODEL // 4
CV_GROUPS = 4
CV_KERNEL = 31
MIX_W = ATT_W + SG_W + CV_W
N_IN = 3 * ATT_W + 2 * SG_W + 2 * CV_W
D_FF = 256 * ((8 * D_MODEL // 3 + 255) // 256)
FFN_KERNEL = 3
EPS = 1e-6
NEG_INF = -1e30

kernel_name = "hybrid_natten_gmlp_conformer_dit"


def _rmsnorm(x, g):
    x32 = x.astype(jnp.float32)
    y = x32 * lax.rsqrt(jnp.mean(x32 * x32, axis=-1, keepdims=True) + EPS)
    return y.astype(x.dtype) * g


def _layernorm(x, g, b):
    x32 = x.astype(jnp.float32)
    mu = jnp.mean(x32, axis=-1, keepdims=True)
    var = jnp.mean(jnp.square(x32 - mu), axis=-1, keepdims=True)
    return ((x32 - mu) * lax.rsqrt(var + EPS)).astype(x.dtype) * g + b


def _group_layernorm(x, g, b, groups):
    shp = x.shape
    x32 = x.astype(jnp.float32).reshape(shp[:-1] + (groups, shp[-1] // groups))
    mu = jnp.mean(x32, axis=-1, keepdims=True)
    var = jnp.mean(jnp.square(x32 - mu), axis=-1, keepdims=True)
    y = ((x32 - mu) * lax.rsqrt(var + EPS)).reshape(shp).astype(x.dtype)
    return y * g + b


def _modulate(h, shift, scale):
    return h * (1 + scale) + shift


def _dwconv(x, w, b):
    k = w.shape[0]
    ch = x.shape[-1]
    y = lax.conv_general_dilated(
        x, w[:, None, :].astype(x.dtype), window_strides=(1,),
        padding=[((k - 1) // 2, k // 2)],
        dimension_numbers=("NWC", "WIO", "NWC"), feature_group_count=ch)
    return y + b


def _heads(t):
    bn, ln, _ = t.shape
    return t.reshape(bn, ln, NA_HEADS, HEAD_DIM).transpose(0, 2, 1, 3)


def _axis_span(length, win, qblk):
    span = min(qblk + win, length)
    start = np.clip(np.arange(length) - win // 2, 0, length - win)
    a = np.clip(np.arange(0, length, qblk) - win // 2, 0, length - span)
    qpos = np.arange(length).reshape(-1, qblk)
    kpos = a[:, None] + np.arange(span)[None, :]
    s_q = start[qpos][:, :, None]
    inwin = (kpos[:, None, :] >= s_q) & (kpos[:, None, :] < s_q + win)
    off = kpos[:, None, :] - qpos[:, :, None]
    return kpos, inwin, off


def _na_index(rows):
    win_r = min(NA_ROWS_MAX, rows)
    qr = next(q for q in (8, 4, 2, 1) if rows % q == 0)
    kr, mr, dr = _axis_span(rows, win_r, qr)
    kc, mc, dc = _axis_span(GRID_W, NA_COLS, NA_QCOLS)
    nbr, nbc = kr.shape[0], kc.shape[0]
    nq = qr * NA_QCOLS
    key_idx = (kr[:, None, :, None] * GRID_W + kc[None, :, None, :]).reshape(nbr * nbc, -1)
    mask = (mr[:, None, :, None, :, None] & mc[None, :, None, :, None, :]).reshape(nbr * nbc, nq, -1)
    ridx = np.clip(dr + NA_ROWS_MAX - 1, 0, 2 * NA_ROWS_MAX - 2)
    cidx = np.clip(dc + NA_COLS - 1, 0, 2 * NA_COLS - 2)
    bias_idx = (ridx[:, None, :, None, :, None] * (2 * NA_COLS - 1)
                + cidx[None, :, None, :, None, :]).reshape(nbr * nbc, nq, -1)
    return qr, key_idx.astype(np.int32), mask, bias_idx.astype(np.int32)


def _neighbourhood_attention(q, k, v, k_ctx, v_ctx, rpb, rows):
    bn, nh, sl, dh = q.shape
    qr, key_idx, mask, bias_idx = _na_index(rows)
    nbr, nbc = rows // qr, GRID_W // NA_QCOLS
    span = key_idx.shape[-1]
    qb = q.reshape(bn, nh, nbr, qr, nbc, NA_QCOLS, dh).transpose(2, 4, 0, 1, 3, 5, 6)
    qb = qb.reshape(nbr * nbc, bn, nh, qr * NA_QCOLS, dh)
    rpb_flat = rpb.reshape(nh, -1)
    scale = dh ** -0.5

    def block(args):
        qblk, kidx, m, bidx = args
        kb = jnp.take(k, kidx, axis=2)
        vb = jnp.take(v, kidx, axis=2)
        s_loc = jnp.einsum("bhqd,bhkd->bhqk", qblk, kb).astype(jnp.float32) * scale
        s_loc = s_loc + jnp.take(rpb_flat, bidx, axis=1).astype(jnp.float32)[None]
        s_loc = jnp.where(m[None, None], s_loc, NEG_INF)
        s_ctx = jnp.einsum("bhqd,bhkd->bhqk", qblk, k_ctx).astype(jnp.float32) * scale
        p = jax.nn.softmax(jnp.concatenate([s_loc, s_ctx], axis=-1), axis=-1).astype(v.dtype)
        return (jnp.einsum("bhqk,bhkd->bhqd", p[..., :span], vb)
                + jnp.einsum("bhqk,bhkd->bhqd", p[..., span:], v_ctx))

    out = lax.map(block, (qb, jnp.asarray(key_idx), jnp.asarray(mask), jnp.asarray(bias_idx)))
    out = out.reshape(nbr, nbc, bn, nh, qr, NA_QCOLS, dh).transpose(2, 0, 4, 1, 5, 3, 6)
    return out.reshape(bn, sl, nh * dh)


def _context_attention(q, k, v):
    s = jnp.einsum("bhqd,bhkd->bhqk", q, k).astype(jnp.float32) * (q.shape[-1] ** -0.5)
    p = jax.nn.softmax(s, axis=-1).astype(v.dtype)
    o = jnp.einsum("bhqk,bhkd->bhqd", p, v)
    bn, nh, ln, dh = o.shape
    return o.transpose(0, 2, 1, 3).reshape(bn, ln, nh * dh)


def _spatial_gating(u, v, ln_g, ln_b, w_s, b_s):
    u = jax.nn.gelu(u)
    v = _layernorm(jax.nn.gelu(v), ln_g, ln_b)
    bn, ln, ch = v.shape
    vr = v.reshape(bn, ln // SG_CHUNK, SG_CHUNK, SG_GROUPS, ch // SG_GROUPS)
    mixed = jnp.einsum("gpq,bnqgc->bnpgc", w_s, vr) + b_s.T[None, None, :, :, None]
    return u * mixed.reshape(bn, ln, ch)


def _conformer_conv(a, gate, w, b, ng, nb):
    h = a * jax.nn.sigmoid(gate)
    h = _dwconv(h, w, b)
    h = _group_layernorm(h, ng, nb, CV_GROUPS)
    return jax.nn.silu(h)


def _mix_out(att, z, sg_ng, sg_nb, sg_w, sg_b, cv_w, cv_b, cv_ng, cv_nb, w_out):
    o1 = 3 * ATT_W
    o2 = o1 + 2 * SG_W
    sg = _spatial_gating(z[..., o1:o1 + SG_W], z[..., o1 + SG_W:o2], sg_ng, sg_nb, sg_w, sg_b)
    cv = _conformer_conv(z[..., o2:o2 + CV_W], z[..., o2 + CV_W:o2 + 2 * CV_W],
                         cv_w, cv_b, cv_ng, cv_nb)
    return jnp.concatenate([att, sg, cv], axis=-1) @ w_out


def _conv_ffn(h, w_up, conv_w, conv_b, w_down):
    gate, up = jnp.split(h @ w_up, 2, axis=-1)
    gate = _dwconv(gate, conv_w, conv_b)
    return (jax.nn.silu(gate) * up) @ w_down


def setup_inputs(seed: int = 0) -> dict:
    key = jax.random.key(seed)
    ks = jax.random.split(key, 24)
    D = D_MODEL

    def n(k, shape, s):
        return jax.random.normal(k, shape, jnp.float32) * s

    return {
        "x": n(ks[0], (BATCH, SEQ, D), 1.0),
        "c": n(ks[1], (BATCH, D), 1.0),
        "ctx": n(ks[2], (BATCH, CTX_LEN, D), 1.0),
        "c_ctx": n(ks[3], (D,), 1.0),
        "ada_w": n(ks[4], (DEPTH, D, 6 * D), 0.5 * D ** -0.5),
        "ada_b": n(ks[5], (DEPTH, 6 * D), 0.02),
        "norm1_g": 1.0 + n(ks[6], (DEPTH, D), 0.05),
        "w_in": n(ks[7], (DEPTH, D, N_IN), D ** -0.5),
        "na_rpb": n(ks[8], (DEPTH, NA_HEADS, 2 * NA_ROWS_MAX - 1, 2 * NA_COLS - 1), 0.5),
        "sg_norm_g": 1.0 + n(ks[9], (DEPTH, SG_W), 0.05),
        "sg_norm_b": n(ks[10], (DEPTH, SG_W), 0.02),
        "sg_w": n(ks[11], (DEPTH, SG_GROUPS, SG_CHUNK, SG_CHUNK), SG_CHUNK ** -0.5),
        "sg_b": 1.0 + n(ks[12], (DEPTH, SG_GROUPS, SG_CHUNK), 0.1),
        "cv_w": n(ks[13], (DEPTH, CV_KERNEL, CV_W), CV_KERNEL ** -0.5),
        "cv_b": n(ks[14], (DEPTH, CV_W), 0.02),
        "cv_norm_g": 1.0 + n(ks[15], (DEPTH, CV_W), 0.05),
        "cv_norm_b": n(ks[16], (DEPTH, CV_W), 0.02),
        "w_out": n(ks[17], (DEPTH, MIX_W, D), MIX_W ** -0.5),
        "norm2_g": 1.0 + n(ks[18], (DEPTH, D), 0.05),
        "ffn_w_up": n(ks[19], (DEPTH, D, 2 * D_FF), D ** -0.5),
        "ffn_conv_w": n(ks[20], (DEPTH, FFN_KERNEL, D_FF), FFN_KERNEL ** -0.5),
        "ffn_conv_b": n(ks[21], (DEPTH, D_FF), 0.02),
        "ffn_w_down": n(ks[22], (DEPTH, D_FF, D), D_FF ** -0.5),
        "final_norm_g": 1.0 + n(ks[23], (D,), 0.05),
    }


def reference(x, c, ctx, c_ctx, ada_w, ada_b, norm1_g, w_in, na_rpb, sg_norm_g, sg_norm_b,
              sg_w, sg_b, cv_w, cv_b, cv_norm_g, cv_norm_b, w_out, norm2_g, ffn_w_up,
              ffn_conv_w, ffn_conv_b, ffn_w_down, final_norm_g):
    rows = x.shape[1] // GRID_W
    xc = ctx
    for l in range(DEPTH):
        last = l == DEPTH - 1
        mod_l = jnp.split((jax.nn.silu(c) @ ada_w[l] + ada_b[l])[:, None, :], 6, axis=-1)
        mod_c = jnp.split((jax.nn.silu(c_ctx) @ ada_w[l] + ada_b[l])[None, None, :], 6, axis=-1)

        h = _modulate(_rmsnorm(x, norm1_g[l]), mod_l[0], mod_l[1])
        hc = _modulate(_rmsnorm(xc, norm1_g[l]), mod_c[0], mod_c[1])
        z = h @ w_in[l]
        q = _heads(z[..., :ATT_W])
        k = _heads(z[..., ATT_W:2 * ATT_W])
        v = _heads(z[..., 2 * ATT_W:3 * ATT_W])
        if last:
            kvc = hc @ w_in[l][:, ATT_W:3 * ATT_W]
            kc = _heads(kvc[..., :ATT_W])
            vc = _heads(kvc[..., ATT_W:])
        else:
            zc = hc @ w_in[l]
            kc = _heads(zc[..., ATT_W:2 * ATT_W])
            vc = _heads(zc[..., 2 * ATT_W:3 * ATT_W])
        att = _neighbourhood_attention(q, k, v, kc, vc, na_rpb[l], rows)
        y = _mix_out(att, z, sg_norm_g[l], sg_norm_b[l], sg_w[l], sg_b[l], cv_w[l], cv_b[l],
                     cv_norm_g[l], cv_norm_b[l], w_out[l])
        x = x + mod_l[2] * y
        if not last:
            att_c = _context_attention(_heads(zc[..., :ATT_W]), kc, vc)
            yc = _mix_out(att_c, zc, sg_norm_g[l], sg_norm_b[l], sg_w[l], sg_b[l], cv_w[l],
                          cv_b[l], cv_norm_g[l], cv_norm_b[l], w_out[l])
            xc = xc + mod_c[2] * yc

        h2 = _modulate(_rmsnorm(x, norm2_g[l]), mod_l[3], mod_l[4])
        x = x + mod_l[5] * _conv_ffn(h2, ffn_w_up[l], ffn_conv_w[l], ffn_conv_b[l], ffn_w_down[l])
        if not last:
            h2c = _modulate(_rmsnorm(xc, norm2_g[l]), mod_c[3], mod_c[4])
            xc = xc + mod_c[5] * _conv_ffn(h2c, ffn_w_up[l], ffn_conv_w[l], ffn_conv_b[l],
                                           ffn_w_down[l])
    return _rmsnorm(x, final_norm_g)
```

```python
import functools

import jax
import jax.numpy as jnp
import numpy as np
from jax import lax
from jax.experimental import pallas as pl
from jax.experimental.pallas import tpu as pltpu

D_MODEL = 1024
DEPTH = 4
GRID_W = 64
HEAD_DIM = 64
ATT_W = 512
NA_HEADS = 8
NA_ROWS = 8
NA_COLS = 16
SG_W = 256
SG_GROUPS = 4
SG_CHUNK = 128
CV_W = 256
CV_GROUPS = 4
CV_KERNEL = 31
N_IN = 3 * ATT_W + 2 * SG_W + 2 * CV_W
D_FF = 2816
FFN_KERNEL = 3
EPS = 1e-6
NEG_INF = -1e30

LANES = 128
HEAD_PAIRS = ATT_W // LANES
Q_ROWS = 2
K_ROWS = 10
CV_HALO = 16
FFN_HALO = 8
FF_CHUNK = 256
VMEM_LIMIT = 56 * 1024 * 1024

F32 = jnp.float32
BF16 = jnp.bfloat16


def _cparams(sem):
    return pltpu.CompilerParams(dimension_semantics=sem, vmem_limit_bytes=VMEM_LIMIT)


def _full(shape):
    nd = len(shape)
    return pl.BlockSpec(shape, lambda *_: (0,) * nd)


def _ada_kernel(c_ref, w_ref, b_ref, o_ref):
    cs = c_ref[...]
    cs = cs * jax.nn.sigmoid(cs)
    o_ref[0] = jnp.dot(cs.astype(BF16), w_ref[0].astype(BF16),
                       preferred_element_type=F32) + b_ref[0]


def _ada_table(c_all, ada_w, ada_b):
    rows = c_all.shape[0]
    tn = 1536
    return pl.pallas_call(
        _ada_kernel,
        grid=(DEPTH, 6 * D_MODEL // tn),
        in_specs=[
            pl.BlockSpec((rows, D_MODEL), lambda l, j: (0, 0)),
            pl.BlockSpec((1, D_MODEL, tn), lambda l, j: (l, 0, j)),
            pl.BlockSpec((1, 1, tn), lambda l, j: (l, 0, j)),
        ],
        out_specs=pl.BlockSpec((1, rows, tn), lambda l, j: (l, 0, j)),
        out_shape=jax.ShapeDtypeStruct((DEPTH, rows, 6 * D_MODEL), F32),
        compiler_params=_cparams(("arbitrary", "arbitrary")),
        name="ada_table",
    )(c_all, ada_w, ada_b.reshape(DEPTH, 1, 6 * D_MODEL))


def _norm_mod(x, g, shift, scale):
    ms = jnp.mean(x * x, axis=-1, keepdims=True)
    return (x * lax.rsqrt(ms + EPS) * g) * (1.0 + scale) + shift


def _inproj_kernel(x_ref, mod_ref, g_ref, w_ref, qkv_ref, z2_ref):
    h = _norm_mod(x_ref[0], g_ref[...], mod_ref[0, 0:1, :], mod_ref[0, 1:2, :]).astype(BF16)
    q = jnp.dot(h, w_ref[:, 0:ATT_W], preferred_element_type=F32)
    qkv_ref[0, :, 0:ATT_W] = (q * (HEAD_DIM ** -0.5)).astype(BF16)
    kv = jnp.dot(h, w_ref[:, ATT_W:3 * ATT_W], preferred_element_type=F32)
    qkv_ref[0, :, ATT_W:3 * ATT_W] = kv.astype(BF16)
    z2_ref[0] = jnp.dot(h, w_ref[:, 3 * ATT_W:], preferred_element_type=F32)


def _inproj(x, mod, g, w, tm):
    bn, ln, d = x.shape
    return pl.pallas_call(
        _inproj_kernel,
        grid=(bn, ln // tm),
        in_specs=[
            pl.BlockSpec((1, tm, d), lambda b, i: (b, i, 0)),
            pl.BlockSpec((1, 6, d), lambda b, i: (b, 0, 0)),
            _full((1, d)),
            _full((d, N_IN)),
        ],
        out_specs=[
            pl.BlockSpec((1, tm, 3 * ATT_W), lambda b, i: (b, i, 0)),
            pl.BlockSpec((1, tm, 2 * SG_W + 2 * CV_W), lambda b, i: (b, i, 0)),
        ],
        out_shape=[
            jax.ShapeDtypeStruct((bn, ln, 3 * ATT_W), BF16),
            jax.ShapeDtypeStruct((bn, ln, 2 * SG_W + 2 * CV_W), F32),
        ],
        compiler_params=_cparams(("parallel", "parallel")),
        name="inproj",
    )(x, mod, g, w)


def _bias_index():
    cfgs = [(0, 0), (2, 0), (8, 4), (28, 22), (30, 22)]
    qc = np.arange(GRID_W)
    kc = np.arange(GRID_W)
    sc = np.clip(qc - NA_COLS // 2, 0, GRID_W - NA_COLS)
    mc = (kc[None, :] >= sc[:, None]) & (kc[None, :] < sc[:, None] + NA_COLS)
    cidx = np.clip(kc[None, :] - qc[:, None] + NA_COLS - 1, 0, 2 * NA_COLS - 2)
    rows = 2048 // GRID_W
    idx_all, mask_all = [], []
    for r0, a in cfgs:
        qr = r0 + np.arange(Q_ROWS)
        kr = a + np.arange(K_ROWS)
        sr = np.clip(qr - NA_ROWS // 2, 0, rows - NA_ROWS)
        mr = (kr[None, :] >= sr[:, None]) & (kr[None, :] < sr[:, None] + NA_ROWS)
        ridx = np.clip(kr[None, :] - qr[:, None] + NA_ROWS - 1, 0, 2 * NA_ROWS - 2)
        idx = ridx[:, None, :, None] * (2 * NA_COLS - 1) + cidx[None, :, None, :]
        mask = mr[:, None, :, None] & mc[None, :, None, :]
        mask = mask.reshape(Q_ROWS * GRID_W, K_ROWS * GRID_W)
        assert (mask.sum(1) == NA_ROWS * NA_COLS).all()
        idx_all.append(idx.reshape(Q_ROWS * GRID_W, K_ROWS * GRID_W))
        mask_all.append(mask)
    return np.stack(idx_all).astype(np.int32), np.stack(mask_all)


def _bias_table(rpb):
    idx, mask = _bias_index()
    flat = rpb.reshape(NA_HEADS, -1)
    t = jnp.where(mask[None], jnp.take(flat, idx, axis=1), NEG_INF)
    t = t.transpose(1, 0, 2, 3)
    return t.reshape(5, HEAD_PAIRS, 2 * Q_ROWS * GRID_W, K_ROWS * GRID_W)


def _stack_heads(q2):
    lane = lax.broadcasted_iota(jnp.int32, q2.shape, 1)
    zero = jnp.zeros_like(q2)
    return jnp.concatenate([jnp.where(lane < HEAD_DIM, q2, zero),
                            jnp.where(lane < HEAD_DIM, zero, q2)], axis=0)


def _unstack_heads(o, n):
    lane = lax.broadcasted_iota(jnp.int32, (n, LANES), 1)
    return jnp.where(lane < HEAD_DIM, o[0:n], o[n:2 * n])


_NT = (((1,), (1,)), ((), ()))


def _attn_kernel(q_ref, k_ref, v_ref, kc_ref, vc_ref, tab_ref, o_ref):
    nq = Q_ROWS * GRID_W
    nk = K_ROWS * GRID_W
    nrb = q_ref.shape[1] // nq
    rows = q_ref.shape[1] // GRID_W

    def body(rb, carry):
        a = jnp.clip(Q_ROWS * rb - NA_ROWS // 2, 0, rows - K_ROWS)
        cfg = jnp.where(rb < 2, rb, jnp.where(rb < nrb - 2, 2, rb - (nrb - 5)))
        q0 = pl.multiple_of(rb * nq, nq)
        k0 = pl.multiple_of(a * GRID_W, GRID_W)
        for p in range(HEAD_PAIRS):
            cols = slice(p * LANES, (p + 1) * LANES)
            qs = _stack_heads(q_ref[0, pl.ds(q0, nq), cols])
            s_loc = lax.dot_general(qs, k_ref[0, pl.ds(k0, nk), cols], _NT,
                                    preferred_element_type=F32) + tab_ref[cfg, p]
            s_ctx = lax.dot_general(qs, kc_ref[0, :, cols], _NT, preferred_element_type=F32)
            m = jnp.maximum(jnp.max(s_loc, axis=-1, keepdims=True),
                            jnp.max(s_ctx, axis=-1, keepdims=True))
            p_loc = jnp.exp(s_loc - m)
            p_ctx = jnp.exp(s_ctx - m)
            den = jnp.sum(p_loc, axis=-1, keepdims=True) + jnp.sum(p_ctx, axis=-1, keepdims=True)
            o = jnp.dot(p_loc.astype(BF16), v_ref[0, pl.ds(k0, nk), cols],
                        preferred_element_type=F32)
            o = o + jnp.dot(p_ctx.astype(BF16), vc_ref[0, :, cols], preferred_element_type=F32)
            o_ref[0, pl.ds(q0, nq), cols] = _unstack_heads(o / den, nq).astype(BF16)
        return carry

    lax.fori_loop(0, nrb, body, 0)


def _attention(qkv, qkv_c, table):
    bn, ln, _ = qkv.shape
    lc = qkv_c.shape[1]
    return pl.pallas_call(
        _attn_kernel,
        grid=(bn,),
        in_specs=[
            pl.BlockSpec((1, ln, ATT_W), lambda b: (b, 0, 0)),
            pl.BlockSpec((1, ln, ATT_W), lambda b: (b, 0, 1)),
            pl.BlockSpec((1, ln, ATT_W), lambda b: (b, 0, 2)),
            pl.BlockSpec((1, lc, ATT_W), lambda b: (b, 0, 1)),
            pl.BlockSpec((1, lc, ATT_W), lambda b: (b, 0, 2)),
            pl.BlockSpec(table.shape, lambda b: (0, 0, 0, 0), pipeline_mode=pl.Buffered(1)),
        ],
        out_specs=pl.BlockSpec((1, ln, ATT_W), lambda b: (b, 0, 0)),
        out_shape=jax.ShapeDtypeStruct((bn, ln, ATT_W), BF16),
        compiler_params=_cparams(("parallel",)),
        name="attention",
    )(qkv, qkv, qkv, qkv_c, qkv_c, table)


def _ctx_attn_kernel(q_ref, k_ref, v_ref, o_ref):
    n = q_ref.shape[1]
    for p in range(HEAD_PAIRS):
        cols = slice(p * LANES, (p + 1) * LANES)
        qs = _stack_heads(q_ref[0, :, cols])
        s = lax.dot_general(qs, k_ref[0, :, cols], _NT, preferred_element_type=F32)
        e = jnp.exp(s - jnp.max(s, axis=-1, keepdims=True))
        den = jnp.sum(e, axis=-1, keepdims=True)
        o = jnp.dot(e.astype(BF16), v_ref[0, :, cols], preferred_element_type=F32)
        o_ref[0, :, cols] = _unstack_heads(o / den, n).astype(BF16)


def _ctx_attention(qkv_c):
    bn, lc, _ = qkv_c.shape
    return pl.pallas_call(
        _ctx_attn_kernel,
        grid=(bn,),
        in_specs=[pl.BlockSpec((1, lc, ATT_W), lambda b, j=j: (b, 0, j)) for j in range(3)],
        out_specs=pl.BlockSpec((1, lc, ATT_W), lambda b: (b, 0, 0)),
        out_shape=jax.ShapeDtypeStruct((bn, lc, ATT_W), BF16),
        compiler_params=_cparams(("parallel",)),
        name="ctx_attention",
    )(qkv_c, qkv_c, qkv_c)


def _mix_kernel(att_ref, z_ref, zp_ref, zn_ref, x_ref, mod_ref, sgg_ref, sgb_ref, ws_ref,
                sbias_ref, cvw_ref, cvb_ref, cvg_ref, cvnb_ref, gmat_ref, wout_ref,
                o_ref, h_scr):
    tm = x_ref.shape[1]
    i = pl.program_id(1)
    nt = pl.num_programs(1)

    u = jax.nn.gelu(z_ref[0, :, 0:SG_W])
    v = jax.nn.gelu(z_ref[0, :, SG_W:2 * SG_W])
    mu = jnp.mean(v, axis=-1, keepdims=True)
    vc = v - mu
    var = jnp.mean(vc * vc, axis=-1, keepdims=True)
    vn = ((vc * lax.rsqrt(var + EPS)) * sgg_ref[...] + sgb_ref[...]).astype(BF16)
    lane = lax.broadcasted_iota(jnp.int32, (SG_CHUNK, SG_W), 1)
    gw = SG_W // SG_GROUPS
    mixed = []
    for c in range(tm // SG_CHUNK):
        r = jnp.dot(ws_ref[...], vn[c * SG_CHUNK:(c + 1) * SG_CHUNK], preferred_element_type=F32)
        m = r[0:SG_CHUNK]
        for g in range(1, SG_GROUPS):
            m = jnp.where(lane >= g * gw, r[g * SG_CHUNK:(g + 1) * SG_CHUNK], m)
        mixed.append(m + sbias_ref[...])
    sg = u * jnp.concatenate(mixed, axis=0)

    a0 = 2 * SG_W
    g0 = a0 + CV_W
    h_scr[CV_HALO:CV_HALO + tm, :] = z_ref[0, :, a0:g0] * jax.nn.sigmoid(z_ref[0, :, g0:g0 + CV_W])
    hp = zp_ref[0, :, 0:CV_W] * jax.nn.sigmoid(zp_ref[0, :, CV_W:2 * CV_W])
    hn = zn_ref[0, :, 0:CV_W] * jax.nn.sigmoid(zn_ref[0, :, CV_W:2 * CV_W])
    h_scr[0:CV_HALO, :] = jnp.where(i > 0, hp, 0.0)
    h_scr[CV_HALO + tm:, :] = jnp.where(i < nt - 1, hn, 0.0)
    pad = CV_KERNEL // 2
    y = jnp.zeros((tm, CV_W), F32) + cvb_ref[...]
    for k in range(CV_KERNEL):
        y = y + cvw_ref[k:k + 1, :] * h_scr[pl.ds(CV_HALO - pad + k, tm), :]

    def group_mean(t):
        hi = t.astype(BF16)
        lo = (t - hi.astype(F32)).astype(BF16)
        return (jnp.dot(hi, gmat_ref[...], preferred_element_type=F32)
                + jnp.dot(lo, gmat_ref[...], preferred_element_type=F32))

    yc = y - group_mean(y)
    yn = yc * lax.rsqrt(group_mean(yc * yc) + EPS) * cvg_ref[...] + cvnb_ref[...]
    cv = yn * jax.nn.sigmoid(yn)

    out = jnp.dot(att_ref[0], wout_ref[0:ATT_W, :], preferred_element_type=F32)
    out = out + jnp.dot(sg.astype(BF16), wout_ref[ATT_W:ATT_W + SG_W, :],
                        preferred_element_type=F32)
    out = out + jnp.dot(cv.astype(BF16), wout_ref[ATT_W + SG_W:, :], preferred_element_type=F32)
    o_ref[0] = x_ref[0] + mod_ref[0, 2:3, :] * out


def _mix(att, z2, x, mod, sgg, sgb, ws, sbias, cvw, cvb, cvg, cvnb, gmat, wout, tm):
    bn, ln, d = x.shape
    nh = tm // CV_HALO
    last = ln // CV_HALO - 1
    zw = 2 * SG_W + 2 * CV_W
    return pl.pallas_call(
        _mix_kernel,
        grid=(bn, ln // tm),
        in_specs=[
            pl.BlockSpec((1, tm, ATT_W), lambda b, i: (b, i, 0)),
            pl.BlockSpec((1, tm, zw), lambda b, i: (b, i, 0)),
            pl.BlockSpec((1, CV_HALO, 2 * CV_W), lambda b, i: (b, jnp.maximum(i * nh - 1, 0), 1)),
            pl.BlockSpec((1, CV_HALO, 2 * CV_W),
                         lambda b, i: (b, jnp.minimum((i + 1) * nh, last), 1)),
            pl.BlockSpec((1, tm, d), lambda b, i: (b, i, 0)),
            pl.BlockSpec((1, 6, d), lambda b, i: (b, 0, 0)),
            _full((1, SG_W)), _full((1, SG_W)),
            _full((SG_GROUPS * SG_CHUNK, SG_CHUNK)),
            _full((SG_CHUNK, SG_W)),
            _full((CV_KERNEL, CV_W)), _full((1, CV_W)), _full((1, CV_W)), _full((1, CV_W)),
            _full((CV_W, CV_W)),
            _full((d, d)),
        ],
        out_specs=pl.BlockSpec((1, tm, d), lambda b, i: (b, i, 0)),
        out_shape=jax.ShapeDtypeStruct((bn, ln, d), F32),
        scratch_shapes=[pltpu.VMEM((tm + 2 * CV_HALO, CV_W), F32)],
        compiler_params=_cparams(("parallel", "parallel")),
        name="mix_out",
    )(att, z2, z2, z2, x, mod, sgg, sgb, ws, sbias, cvw, cvb, cvg, cvnb, gmat, wout)


def _ffn_kernel(x_ref, xp_ref, xn_ref, mod_ref, g_ref, wup_ref, cw_ref, cb_ref, wdn_ref, fg_ref,
                o_ref, g_scr, *, final):
    tm = x_ref.shape[1]
    i = pl.program_id(1)
    nt = pl.num_programs(1)
    shift, scale = mod_ref[0, 3:4, :], mod_ref[0, 4:5, :]
    xm = x_ref[0]
    hm = _norm_mod(xm, g_ref[...], shift, scale).astype(BF16)
    hp = jnp.where(i > 0, _norm_mod(xp_ref[0], g_ref[...], shift, scale), 0.0).astype(BF16)
    hn = jnp.where(i < nt - 1, _norm_mod(xn_ref[0], g_ref[...], shift, scale), 0.0).astype(BF16)
    hh = jnp.concatenate([hp, hm, hn], axis=0)

    acc = jnp.zeros((tm, D_MODEL), F32)
    for j in range(D_FF // FF_CHUNK):
        c0 = j * FF_CHUNK
        g_scr[...] = jnp.dot(hh, wup_ref[:, c0:c0 + FF_CHUNK], preferred_element_type=F32)
        up = jnp.dot(hm, wup_ref[:, D_FF + c0:D_FF + c0 + FF_CHUNK], preferred_element_type=F32)
        gc = cb_ref[:, c0:c0 + FF_CHUNK]
        for k in range(FFN_KERNEL):
            gc = gc + cw_ref[k:k + 1, c0:c0 + FF_CHUNK] * g_scr[pl.ds(FFN_HALO - 1 + k, tm), :]
        act = (gc * jax.nn.sigmoid(gc) * up).astype(BF16)
        acc = acc + jnp.dot(act, wdn_ref[c0:c0 + FF_CHUNK, :], preferred_element_type=F32)
    xo = xm + mod_ref[0, 5:6, :] * acc
    if final:
        ms = jnp.mean(xo * xo, axis=-1, keepdims=True)
        xo = xo * lax.rsqrt(ms + EPS) * fg_ref[...]
    o_ref[0] = xo


def _ffn(x, mod, g, wup, cw, cb, wdn, fg, tm, final):
    bn, ln, d = x.shape
    nh = tm // FFN_HALO
    last = ln // FFN_HALO - 1
    single = dict(pipeline_mode=pl.Buffered(1))
    return pl.pallas_call(
        functools.partial(_ffn_kernel, final=final),
        grid=(bn, ln // tm),
        in_specs=[
            pl.BlockSpec((1, tm, d), lambda b, i: (b, i, 0)),
            pl.BlockSpec((1, FFN_HALO, d), lambda b, i: (b, jnp.maximum(i * nh - 1, 0), 0)),
            pl.BlockSpec((1, FFN_HALO, d), lambda b, i: (b, jnp.minimum((i + 1) * nh, last), 0)),
            pl.BlockSpec((1, 6, d), lambda b, i: (b, 0, 0)),
            _full((1, d)),
            pl.BlockSpec((d, 2 * D_FF), lambda b, i: (0, 0), **single),
            _full((FFN_KERNEL, D_FF)), _full((1, D_FF)),
            pl.BlockSpec((D_FF, d), lambda b, i: (0, 0), **single),
            _full((1, d)),
        ],
        out_specs=pl.BlockSpec((1, tm, d), lambda b, i: (b, i, 0)),
        out_shape=jax.ShapeDtypeStruct((bn, ln, d), F32),
        scratch_shapes=[pltpu.VMEM((tm + 2 * FFN_HALO, FF_CHUNK), F32)],
        compiler_params=_cparams(("parallel", "parallel")),
        name="conv_ffn",
    )(x, x, x, mod, g, wup, cw, cb, wdn, fg)


def kernel(x, c, ctx, c_ctx, ada_w, ada_b, norm1_g, w_in, na_rpb, sg_norm_g, sg_norm_b, sg_w,
           sg_b, cv_w, cv_b, cv_norm_g, cv_norm_b, w_out, norm2_g, ffn_w_up, ffn_conv_w,
           ffn_conv_b, ffn_w_down, final_norm_g):
    bn, ln, d = x.shape
    lc = ctx.shape[1]
    assert (d, ln, ln // GRID_W) == (D_MODEL, 2048, 32) and lc % SG_CHUNK == 0

    rows = -(-(bn + 1) // 8) * 8
    c_all = jnp.zeros((rows, d), F32).at[:bn].set(c).at[bn].set(c_ctx)
    mods = _ada_table(c_all, ada_w, ada_b)

    w_in_b = w_in.astype(BF16)
    w_out_b = w_out.astype(BF16)
    w_up_b = ffn_w_up.astype(BF16)
    w_dn_b = ffn_w_down.astype(BF16)
    ws_b = sg_w.reshape(DEPTH, SG_GROUPS * SG_CHUNK, SG_CHUNK).astype(BF16)
    sbias = jnp.repeat(sg_b.transpose(0, 2, 1), SG_W // SG_GROUPS, axis=2)
    grp = np.arange(CV_W) // (CV_W // CV_GROUPS)
    gmat = jnp.asarray((grp[:, None] == grp[None, :]) / (CV_W // CV_GROUPS), BF16)
    fg = final_norm_g.reshape(1, d)

    tm, tmc = 512, lc
    xc = ctx
    for l in range(DEPTH):
        last = l == DEPTH - 1
        mod_l = mods[l, :bn].reshape(bn, 6, d)
        mod_c = jnp.broadcast_to(mods[l, bn].reshape(1, 6, d), (bn, 6, d))
        g1 = norm1_g[l].reshape(1, d)
        g2 = norm2_g[l].reshape(1, d)
        mix_w = (sg_norm_g[l].reshape(1, SG_W), sg_norm_b[l].reshape(1, SG_W), ws_b[l], sbias[l],
                 cv_w[l], cv_b[l].reshape(1, CV_W), cv_norm_g[l].reshape(1, CV_W),
                 cv_norm_b[l].reshape(1, CV_W), gmat, w_out_b[l])
        ffn_w = (w_up_b[l], ffn_conv_w[l], ffn_conv_b[l].reshape(1, D_FF), w_dn_b[l], fg)

        qkv_c, z2_c = _inproj(xc, mod_c, g1, w_in_b[l], tmc)
        qkv, z2 = _inproj(x, mod_l, g1, w_in_b[l], tm)
        att = _attention(qkv, qkv_c, _bias_table(na_rpb[l]))
        x = _mix(att, z2, x, mod_l, *mix_w, tm)
        x = _ffn(x, mod_l, g2, *ffn_w, tm, last)
        if not last:
            att_c = _ctx_attention(qkv_c)
            xc = _mix(att_c, z2_c, xc, mod_c, *mix_w, tmc)
            xc = _ffn(xc, mod_c, g2, *ffn_w, tmc, False)
    return x
```

```python
import functools

import jax
import jax.numpy as jnp
import numpy as np
from jax import lax
from jax.experimental import pallas as pl
from jax.experimental.pallas import tpu as pltpu

D_MODEL = 1024
DEPTH = 4
GRID_W = 64
HEAD_DIM = 64
ATT_W = 512
NA_HEADS = 8
NA_ROWS = 8
NA_COLS = 16
SG_W = 256
SG_GROUPS = 4
SG_CHUNK = 128
CV_W = 256
CV_GROUPS = 4
CV_KERNEL = 31
N_IN = 3 * ATT_W + 2 * SG_W + 2 * CV_W
D_FF = 2816
FFN_KERNEL = 3
EPS = 1e-6
NEG_INF = -1e30

LANES = 128
HEAD_PAIRS = ATT_W // LANES
Q_ROWS = 2
K_ROWS = 10
CV_HALO = 16
FFN_HALO = 8
FF_CHUNK = 256
VMEM_LIMIT = 56 * 1024 * 1024

F32 = jnp.float32
BF16 = jnp.bfloat16


def _cparams(sem):
    return pltpu.CompilerParams(dimension_semantics=sem, vmem_limit_bytes=VMEM_LIMIT)


def _full(shape):
    nd = len(shape)
    return pl.BlockSpec(shape, lambda *_: (0,) * nd)


def _ada_kernel(c_ref, w_ref, b_ref, o_ref):
    cs = c_ref[...]
    cs = cs * jax.nn.sigmoid(cs)
    o_ref[0] = jnp.dot(cs.astype(BF16), w_ref[0].astype(BF16),
                       preferred_element_type=F32) + b_ref[0]


def _ada_table(c_all, ada_w, ada_b):
    rows = c_all.shape[0]
    tn = 1536
    return pl.pallas_call(
        _ada_kernel,
        grid=(DEPTH, 6 * D_MODEL // tn),
        in_specs=[
            pl.BlockSpec((rows, D_MODEL), lambda l, j: (0, 0)),
            pl.BlockSpec((1, D_MODEL, tn), lambda l, j: (l, 0, j)),
            pl.BlockSpec((1, 1, tn), lambda l, j: (l, 0, j)),
        ],
        out_specs=pl.BlockSpec((1, rows, tn), lambda l, j: (l, 0, j)),
        out_shape=jax.ShapeDtypeStruct((DEPTH, rows, 6 * D_MODEL), F32),
        compiler_params=_cparams(("arbitrary", "arbitrary")),
        name="ada_table",
    )(c_all, ada_w, ada_b.reshape(DEPTH, 1, 6 * D_MODEL))


def _norm_mod(x, g, shift, scale):
    ms = jnp.mean(x * x, axis=-1, keepdims=True)
    return (x * lax.rsqrt(ms + EPS) * g) * (1.0 + scale) + shift


def _inproj_kernel(x_ref, mod_ref, g_ref, w_ref, qkv_ref, z2_ref):
    h = _norm_mod(x_ref[0], g_ref[...], mod_ref[0, 0:1, :], mod_ref[0, 1:2, :]).astype(BF16)
    q = jnp.dot(h, w_ref[:, 0:ATT_W], preferred_element_type=F32)
    qkv_ref[0, :, 0:ATT_W] = (q * (HEAD_DIM ** -0.5)).astype(BF16)
    kv = jnp.dot(h, w_ref[:, ATT_W:3 * ATT_W], preferred_element_type=F32)
    qkv_ref[0, :, ATT_W:3 * ATT_W] = kv.astype(BF16)
    z2_ref[0] = jnp.dot(h, w_ref[:, 3 * ATT_W:], preferred_element_type=F32)


def _inproj(x, mod, g, w, tm):
    bn, ln, d = x.shape
    return pl.pallas_call(
        _inproj_kernel,
        grid=(bn, ln // tm),
        in_specs=[
            pl.BlockSpec((1, tm, d), lambda b, i: (b, i, 0)),
            pl.BlockSpec((1, 6, d), lambda b, i: (b, 0, 0)),
            _full((1, d)),
            _full((d, N_IN)),
        ],
        out_specs=[
            pl.BlockSpec((1, tm, 3 * ATT_W), lambda b, i: (b, i, 0)),
            pl.BlockSpec((1, tm, 2 * SG_W + 2 * CV_W), lambda b, i: (b, i, 0)),
        ],
        out_shape=[
            jax.ShapeDtypeStruct((bn, ln, 3 * ATT_W), BF16),
            jax.ShapeDtypeStruct((bn, ln, 2 * SG_W + 2 * CV_W), F32),
        ],
        compiler_params=_cparams(("parallel", "parallel")),
        name="inproj",
    )(x, mod, g, w)


GRID_ROWS = 2048 // GRID_W
RPB_W = 2 * NA_COLS - 1
RPB_PAD = 512
ROW_CFGS = ((0, 0), (2, 0), (8, 4), (GRID_ROWS - 4, GRID_ROWS - K_ROWS),
            (GRID_ROWS - 2, GRID_ROWS - K_ROWS))


def _table_kernel(rpb_ref, o_ref):
    r_base = pl.program_id(0) * NA_HEADS + pl.program_id(1) * 2
    shape = (GRID_W, LANES)
    qc = lax.broadcasted_iota(jnp.int32, shape, 0)
    lane = lax.broadcasted_iota(jnp.int32, shape, 1)
    low = lane < GRID_W
    kc = jnp.where(low, lane, lane - GRID_W)
    off = kc - qc + (NA_COLS - 1)
    sc = jnp.clip(qc - NA_COLS // 2, 0, GRID_W - NA_COLS)
    in_cols = (kc >= sc) & (kc < sc + NA_COLS)
    neg = jnp.full(shape, NEG_INF, F32)
    for hh in range(2):
        toep = []
        for u in range(2 * NA_ROWS - 1):
            acc = jnp.zeros(shape, F32)
            for w in range(RPB_W):
                acc = jnp.where(off == w, rpb_ref[r_base + hh, u * RPB_W + w], acc)
            toep.append(jnp.where(in_cols, acc, neg))
        for cfg, (r0, a) in enumerate(ROW_CFGS):
            for ql in range(Q_ROWS):
                qr = r0 + ql
                sr = min(max(qr - NA_ROWS // 2, 0), GRID_ROWS - NA_ROWS)
                for kp in range(K_ROWS // 2):
                    halves = []
                    for kr in (a + 2 * kp, a + 2 * kp + 1):
                        inside = sr <= kr < sr + NA_ROWS
                        halves.append(toep[kr - qr + NA_ROWS - 1] if inside else neg)
                    r_lo = hh * Q_ROWS * GRID_W + ql * GRID_W
                    o_ref[0, cfg, 0, r_lo:r_lo + GRID_W, kp * LANES:(kp + 1) * LANES] = (
                        jnp.where(low, halves[0], halves[1]))


def _bias_tables(na_rpb):
    flat = na_rpb.reshape(DEPTH * NA_HEADS, -1)
    flat = jnp.pad(flat, ((0, 0), (0, RPB_PAD - flat.shape[1])))
    nq2, nk = 2 * Q_ROWS * GRID_W, K_ROWS * GRID_W
    return pl.pallas_call(
        _table_kernel,
        grid=(DEPTH, HEAD_PAIRS),
        in_specs=[pl.BlockSpec(memory_space=pltpu.SMEM)],
        out_specs=pl.BlockSpec((1, len(ROW_CFGS), 1, nq2, nk), lambda l, p: (l, 0, p, 0, 0)),
        out_shape=jax.ShapeDtypeStruct((DEPTH, len(ROW_CFGS), HEAD_PAIRS, nq2, nk), F32),
        compiler_params=_cparams(("arbitrary", "arbitrary")),
        name="bias_tables",
    )(flat)


def _stack_heads(q2):
    lane = lax.broadcasted_iota(jnp.int32, q2.shape, 1)
    zero = jnp.zeros_like(q2)
    return jnp.concatenate([jnp.where(lane < HEAD_DIM, q2, zero),
                            jnp.where(lane < HEAD_DIM, zero, q2)], axis=0)


def _unstack_heads(o, n):
    lane = lax.broadcasted_iota(jnp.int32, (n, LANES), 1)
    return jnp.where(lane < HEAD_DIM, o[0:n], o[n:2 * n])


_NT = (((1,), (1,)), ((), ()))


def _attn_kernel(q_ref, k_ref, v_ref, kc_ref, vc_ref, tab_ref, o_ref):
    nq = Q_ROWS * GRID_W
    nk = K_ROWS * GRID_W
    nrb = q_ref.shape[1] // nq
    rows = q_ref.shape[1] // GRID_W

    def body(rb, carry):
        a = jnp.clip(Q_ROWS * rb - NA_ROWS // 2, 0, rows - K_ROWS)
        cfg = jnp.where(rb < 2, rb, jnp.where(rb < nrb - 2, 2, rb - (nrb - 5)))
        q0 = pl.multiple_of(rb * nq, nq)
        k0 = pl.multiple_of(a * GRID_W, GRID_W)
        for p in range(HEAD_PAIRS):
            cols = slice(p * LANES, (p + 1) * LANES)
            qs = _stack_heads(q_ref[0, pl.ds(q0, nq), cols])
            s_loc = lax.dot_general(qs, k_ref[0, pl.ds(k0, nk), cols], _NT,
                                    preferred_element_type=F32) + tab_ref[0, cfg, p]
            s_ctx = lax.dot_general(qs, kc_ref[0, :, cols], _NT, preferred_element_type=F32)
            m = jnp.maximum(jnp.max(s_loc, axis=-1, keepdims=True),
                            jnp.max(s_ctx, axis=-1, keepdims=True))
            p_loc = jnp.exp(s_loc - m)
            p_ctx = jnp.exp(s_ctx - m)
            den = jnp.sum(p_loc, axis=-1, keepdims=True) + jnp.sum(p_ctx, axis=-1, keepdims=True)
            o = jnp.dot(p_loc.astype(BF16), v_ref[0, pl.ds(k0, nk), cols],
                        preferred_element_type=F32)
            o = o + jnp.dot(p_ctx.astype(BF16), vc_ref[0, :, cols], preferred_element_type=F32)
            o_ref[0, pl.ds(q0, nq), cols] = _unstack_heads(o / den, nq).astype(BF16)
        return carry

    lax.fori_loop(0, nrb, body, 0)


def _attention(qkv, qkv_c, tables, layer):
    bn, ln, _ = qkv.shape
    lc = qkv_c.shape[1]
    return pl.pallas_call(
        _attn_kernel,
        grid=(bn,),
        in_specs=[
            pl.BlockSpec((1, ln, ATT_W), lambda b: (b, 0, 0)),
            pl.BlockSpec((1, ln, ATT_W), lambda b: (b, 0, 1)),
            pl.BlockSpec((1, ln, ATT_W), lambda b: (b, 0, 2)),
            pl.BlockSpec((1, lc, ATT_W), lambda b: (b, 0, 1)),
            pl.BlockSpec((1, lc, ATT_W), lambda b: (b, 0, 2)),
            pl.BlockSpec((1,) + tables.shape[1:], lambda b: (layer, 0, 0, 0, 0),
                         pipeline_mode=pl.Buffered(1)),
        ],
        out_specs=pl.BlockSpec((1, ln, ATT_W), lambda b: (b, 0, 0)),
        out_shape=jax.ShapeDtypeStruct((bn, ln, ATT_W), BF16),
        compiler_params=_cparams(("parallel",)),
        name="attention",
    )(qkv, qkv, qkv, qkv_c, qkv_c, tables)


def _ctx_attn_kernel(q_ref, k_ref, v_ref, o_ref):
    n = q_ref.shape[1]
    for p in range(HEAD_PAIRS):
        cols = slice(p * LANES, (p + 1) * LANES)
        qs = _stack_heads(q_ref[0, :, cols])
        s = lax.dot_general(qs, k_ref[0, :, cols], _NT, preferred_element_type=F32)
        e = jnp.exp(s - jnp.max(s, axis=-1, keepdims=True))
        den = jnp.sum(e, axis=-1, keepdims=True)
        o = jnp.dot(e.astype(BF16), v_ref[0, :, cols], preferred_element_type=F32)
        o_ref[0, :, cols] = _unstack_heads(o / den, n).astype(BF16)


def _ctx_attention(qkv_c):
    bn, lc, _ = qkv_c.shape
    return pl.pallas_call(
        _ctx_attn_kernel,
        grid=(bn,),
        in_specs=[pl.BlockSpec((1, lc, ATT_W), lambda b, j=j: (b, 0, j)) for j in range(3)],
        out_specs=pl.BlockSpec((1, lc, ATT_W), lambda b: (b, 0, 0)),
        out_shape=jax.ShapeDtypeStruct((bn, lc, ATT_W), BF16),
        compiler_params=_cparams(("parallel",)),
        name="ctx_attention",
    )(qkv_c, qkv_c, qkv_c)


def _mix_kernel(att_ref, z_ref, zp_ref, zn_ref, x_ref, mod_ref, sgg_ref, sgb_ref, ws_ref,
                sbias_ref, cvw_ref, cvb_ref, cvg_ref, cvnb_ref, gmat_ref, wout_ref,
                o_ref, h_scr):
    tm = x_ref.shape[1]
    i = pl.program_id(1)
    nt = pl.num_programs(1)

    u = jax.nn.gelu(z_ref[0, :, 0:SG_W])
    v = jax.nn.gelu(z_ref[0, :, SG_W:2 * SG_W])
    mu = jnp.mean(v, axis=-1, keepdims=True)
    vc = v - mu
    var = jnp.mean(vc * vc, axis=-1, keepdims=True)
    vn = ((vc * lax.rsqrt(var + EPS)) * sgg_ref[...] + sgb_ref[...]).astype(BF16)
    lane = lax.broadcasted_iota(jnp.int32, (SG_CHUNK, SG_W), 1)
    gw = SG_W // SG_GROUPS
    mixed = []
    for c in range(tm // SG_CHUNK):
        r = jnp.dot(ws_ref[...], vn[c * SG_CHUNK:(c + 1) * SG_CHUNK], preferred_element_type=F32)
        m = r[0:SG_CHUNK]
        for g in range(1, SG_GROUPS):
            m = jnp.where(lane >= g * gw, r[g * SG_CHUNK:(g + 1) * SG_CHUNK], m)
        mixed.append(m + sbias_ref[...])
    sg = u * jnp.concatenate(mixed, axis=0)

    a0 = 2 * SG_W
    g0 = a0 + CV_W
    h_scr[CV_HALO:CV_HALO + tm, :] = z_ref[0, :, a0:g0] * jax.nn.sigmoid(z_ref[0, :, g0:g0 + CV_W])
    hp = zp_ref[0, :, 0:CV_W] * jax.nn.sigmoid(zp_ref[0, :, CV_W:2 * CV_W])
    hn = zn_ref[0, :, 0:CV_W] * jax.nn.sigmoid(zn_ref[0, :, CV_W:2 * CV_W])
    h_scr[0:CV_HALO, :] = jnp.where(i > 0, hp, 0.0)
    h_scr[CV_HALO + tm:, :] = jnp.where(i < nt - 1, hn, 0.0)
    pad = CV_KERNEL // 2
    y = jnp.zeros((tm, CV_W), F32) + cvb_ref[...]
    for k in range(CV_KERNEL):
        y = y + cvw_ref[k:k + 1, :] * h_scr[pl.ds(CV_HALO - pad + k, tm), :]

    def group_mean(t):
        hi = t.astype(BF16)
        lo = (t - hi.astype(F32)).astype(BF16)
        return (jnp.dot(hi, gmat_ref[...], preferred_element_type=F32)
                + jnp.dot(lo, gmat_ref[...], preferred_element_type=F32))

    yc = y - group_mean(y)
    yn = yc * lax.rsqrt(group_mean(yc * yc) + EPS) * cvg_ref[...] + cvnb_ref[...]
    cv = yn * jax.nn.sigmoid(yn)

    out = jnp.dot(att_ref[0], wout_ref[0:ATT_W, :], preferred_element_type=F32)
    out = out + jnp.dot(sg.astype(BF16), wout_ref[ATT_W:ATT_W + SG_W, :],
                        preferred_element_type=F32)
    out = out + jnp.dot(cv.astype(BF16), wout_ref[ATT_W + SG_W:, :], preferred_element_type=F32)
    o_ref[0] = x_ref[0] + mod_ref[0, 2:3, :] * out


def _mix(att, z2, x, mod, sgg, sgb, ws, sbias, cvw, cvb, cvg, cvnb, gmat, wout, tm):
    bn, ln, d = x.shape
    nh = tm // CV_HALO
    last = ln // CV_HALO - 1
    zw = 2 * SG_W + 2 * CV_W
    return pl.pallas_call(
        _mix_kernel,
        grid=(bn, ln // tm),
        in_specs=[
            pl.BlockSpec((1, tm, ATT_W), lambda b, i: (b, i, 0)),
            pl.BlockSpec((1, tm, zw), lambda b, i: (b, i, 0)),
            pl.BlockSpec((1, CV_HALO, 2 * CV_W), lambda b, i: (b, jnp.maximum(i * nh - 1, 0), 1)),
            pl.BlockSpec((1, CV_HALO, 2 * CV_W),
                         lambda b, i: (b, jnp.minimum((i + 1) * nh, last), 1)),
            pl.BlockSpec((1, tm, d), lambda b, i: (b, i, 0)),
            pl.BlockSpec((1, 6, d), lambda b, i: (b, 0, 0)),
            _full((1, SG_W)), _full((1, SG_W)),
            _full((SG_GROUPS * SG_CHUNK, SG_CHUNK)),
            _full((SG_CHUNK, SG_W)),
            _full((CV_KERNEL, CV_W)), _full((1, CV_W)), _full((1, CV_W)), _full((1, CV_W)),
            _full((CV_W, CV_W)),
            _full((d, d)),
        ],
        out_specs=pl.BlockSpec((1, tm, d), lambda b, i: (b, i, 0)),
        out_shape=jax.ShapeDtypeStruct((bn, ln, d), F32),
        scratch_shapes=[pltpu.VMEM((tm + 2 * CV_HALO, CV_W), F32)],
        compiler_params=_cparams(("parallel", "parallel")),
        name="mix_out",
    )(att, z2, z2, z2, x, mod, sgg, sgb, ws, sbias, cvw, cvb, cvg, cvnb, gmat, wout)


def _ffn_kernel(x_ref, xp_ref, xn_ref, mod_ref, g_ref, wup_ref, cw_ref, cb_ref, wdn_ref, fg_ref,
                o_ref, g_scr, *, final):
    tm = x_ref.shape[1]
    i = pl.program_id(1)
    nt = pl.num_programs(1)
    shift, scale = mod_ref[0, 3:4, :], mod_ref[0, 4:5, :]
    xm = x_ref[0]
    hm = _norm_mod(xm, g_ref[...], shift, scale).astype(BF16)
    hp = jnp.where(i > 0, _norm_mod(xp_ref[0], g_ref[...], shift, scale), 0.0).astype(BF16)
    hn = jnp.where(i < nt - 1, _norm_mod(xn_ref[0], g_ref[...], shift, scale), 0.0).astype(BF16)
    hh = jnp.concatenate([hp, hm, hn], axis=0)

    acc = jnp.zeros((tm, D_MODEL), F32)
    for j in range(D_FF // FF_CHUNK):
        c0 = j * FF_CHUNK
        g_scr[...] = jnp.dot(hh, wup_ref[:, c0:c0 + FF_CHUNK], preferred_element_type=F32)
        up = jnp.dot(hm, wup_ref[:, D_FF + c0:D_FF + c0 + FF_CHUNK], preferred_element_type=F32)
        gc = cb_ref[:, c0:c0 + FF_CHUNK]
        for k in range(FFN_KERNEL):
            gc = gc + cw_ref[k:k + 1, c0:c0 + FF_CHUNK] * g_scr[pl.ds(FFN_HALO - 1 + k, tm), :]
        act = (gc * jax.nn.sigmoid(gc) * up).astype(BF16)
        acc = acc + jnp.dot(act, wdn_ref[c0:c0 + FF_CHUNK, :], preferred_element_type=F32)
    xo = xm + mod_ref[0, 5:6, :] * acc
    if final:
        ms = jnp.mean(xo * xo, axis=-1, keepdims=True)
        xo = xo * lax.rsqrt(ms + EPS) * fg_ref[...]
    o_ref[0] = xo


def _ffn(x, mod, g, wup, cw, cb, wdn, fg, tm, final):
    bn, ln, d = x.shape
    nh = tm // FFN_HALO
    last = ln // FFN_HALO - 1
    single = dict(pipeline_mode=pl.Buffered(1))
    return pl.pallas_call(
        functools.partial(_ffn_kernel, final=final),
        grid=(bn, ln // tm),
        in_specs=[
            pl.BlockSpec((1, tm, d), lambda b, i: (b, i, 0)),
            pl.BlockSpec((1, FFN_HALO, d), lambda b, i: (b, jnp.maximum(i * nh - 1, 0), 0)),
            pl.BlockSpec((1, FFN_HALO, d), lambda b, i: (b, jnp.minimum((i + 1) * nh, last), 0)),
            pl.BlockSpec((1, 6, d), lambda b, i: (b, 0, 0)),
            _full((1, d)),
            pl.BlockSpec((d, 2 * D_FF), lambda b, i: (0, 0), **single),
            _full((FFN_KERNEL, D_FF)), _full((1, D_FF)),
            pl.BlockSpec((D_FF, d), lambda b, i: (0, 0), **single),
            _full((1, d)),
        ],
        out_specs=pl.BlockSpec((1, tm, d), lambda b, i: (b, i, 0)),
        out_shape=jax.ShapeDtypeStruct((bn, ln, d), F32),
        scratch_shapes=[pltpu.VMEM((tm + 2 * FFN_HALO, FF_CHUNK), F32)],
        compiler_params=_cparams(("parallel", "parallel")),
        name="conv_ffn",
    )(x, x, x, mod, g, wup, cw, cb, wdn, fg)


def kernel(x, c, ctx, c_ctx, ada_w, ada_b, norm1_g, w_in, na_rpb, sg_norm_g, sg_norm_b, sg_w,
           sg_b, cv_w, cv_b, cv_norm_g, cv_norm_b, w_out, norm2_g, ffn_w_up, ffn_conv_w,
           ffn_conv_b, ffn_w_down, final_norm_g):
    bn, ln, d = x.shape
    lc = ctx.shape[1]
    assert (d, ln, ln // GRID_W) == (D_MODEL, 2048, 32) and lc % SG_CHUNK == 0

    rows = -(-(bn + 1) // 8) * 8
    c_all = jnp.zeros((rows, d), F32).at[:bn].set(c).at[bn].set(c_ctx)
    mods = _ada_table(c_all, ada_w, ada_b)
    tables = _bias_tables(na_rpb)

    w_in_b = w_in.astype(BF16)
    w_out_b = w_out.astype(BF16)
    w_up_b = ffn_w_up.astype(BF16)
    w_dn_b = ffn_w_down.astype(BF16)
    ws_b = sg_w.reshape(DEPTH, SG_GROUPS * SG_CHUNK, SG_CHUNK).astype(BF16)
    sbias = jnp.repeat(sg_b.transpose(0, 2, 1), SG_W // SG_GROUPS, axis=2)
    grp = np.arange(CV_W) // (CV_W // CV_GROUPS)
    gmat = jnp.asarray((grp[:, None] == grp[None, :]) / (CV_W // CV_GROUPS), BF16)
    fg = final_norm_g.reshape(1, d)

    tm, tmc = 512, lc
    xc = ctx
    for l in range(DEPTH):
        last = l == DEPTH - 1
        mod_l = mods[l, :bn].reshape(bn, 6, d)
        mod_c = jnp.broadcast_to(mods[l, bn].reshape(1, 6, d), (bn, 6, d))
        g1 = norm1_g[l].reshape(1, d)
        g2 = norm2_g[l].reshape(1, d)
        mix_w = (sg_norm_g[l].reshape(1, SG_W), sg_norm_b[l].reshape(1, SG_W), ws_b[l], sbias[l],
                 cv_w[l], cv_b[l].reshape(1, CV_W), cv_norm_g[l].reshape(1, CV_W),
                 cv_norm_b[l].reshape(1, CV_W), gmat, w_out_b[l])
        ffn_w = (w_up_b[l], ffn_conv_w[l], ffn_conv_b[l].reshape(1, D_FF), w_dn_b[l], fg)

        qkv_c, z2_c = _inproj(xc, mod_c, g1, w_in_b[l], tmc)
        qkv, z2 = _inproj(x, mod_l, g1, w_in_b[l], tm)
        att = _attention(qkv, qkv_c, tables, l)
        x = _mix(att, z2, x, mod_l, *mix_w, tm)
        x = _ffn(x, mod_l, g2, *ffn_w, tm, last)
        if not last:
            att_c = _ctx_attention(qkv_c)
            xc = _mix(att_c, z2_c, xc, mod_c, *mix_w, tmc)
            xc = _ffn(xc, mod_c, g2, *ffn_w, tmc, False)
    return x
```

```python
import functools

import jax
import jax.numpy as jnp
import numpy as np
from jax import lax
from jax.experimental import pallas as pl
from jax.experimental.pallas import tpu as pltpu

D_MODEL = 1024
DEPTH = 4
GRID_W = 64
HEAD_DIM = 64
ATT_W = 512
NA_HEADS = 8
NA_ROWS = 8
NA_COLS = 16
SG_W = 256
SG_GROUPS = 4
SG_CHUNK = 128
CV_W = 256
CV_GROUPS = 4
CV_KERNEL = 31
N_IN = 3 * ATT_W + 2 * SG_W + 2 * CV_W
D_FF = 2816
FFN_KERNEL = 3
EPS = 1e-6
NEG_INF = -1e30

LANES = 128
HEAD_PAIRS = ATT_W // LANES
Q_ROWS = 2
K_ROWS = 10
CV_HALO = 16
FFN_HALO = 8
FF_CHUNK = 256
ATT_UNROLL = 4
VMEM_LIMIT = 56 * 1024 * 1024

F32 = jnp.float32
BF16 = jnp.bfloat16


def _cparams(sem):
    return pltpu.CompilerParams(dimension_semantics=sem, vmem_limit_bytes=VMEM_LIMIT)


def _full(shape):
    nd = len(shape)
    return pl.BlockSpec(shape, lambda *_: (0,) * nd)


def _ada_kernel(c_ref, w_ref, b_ref, o_ref):
    cs = c_ref[...]
    cs = cs * jax.nn.sigmoid(cs)
    o_ref[0] = jnp.dot(cs.astype(BF16), w_ref[0].astype(BF16),
                       preferred_element_type=F32) + b_ref[0]


def _ada_table(c_all, ada_w, ada_b):
    rows = c_all.shape[0]
    tn = 1536
    return pl.pallas_call(
        _ada_kernel,
        grid=(DEPTH, 6 * D_MODEL // tn),
        in_specs=[
            pl.BlockSpec((rows, D_MODEL), lambda l, j: (0, 0)),
            pl.BlockSpec((1, D_MODEL, tn), lambda l, j: (l, 0, j)),
            pl.BlockSpec((1, 1, tn), lambda l, j: (l, 0, j)),
        ],
        out_specs=pl.BlockSpec((1, rows, tn), lambda l, j: (l, 0, j)),
        out_shape=jax.ShapeDtypeStruct((DEPTH, rows, 6 * D_MODEL), F32),
        compiler_params=_cparams(("arbitrary", "arbitrary")),
        name="ada_table",
    )(c_all, ada_w, ada_b.reshape(DEPTH, 1, 6 * D_MODEL))


def _norm_mod(x, g, shift, scale):
    ms = jnp.mean(x * x, axis=-1, keepdims=True)
    return (x * lax.rsqrt(ms + EPS) * g) * (1.0 + scale) + shift


_NT = (((1,), (1,)), ((), ()))


def _inproj_kernel(x_ref, mod_ref, g_ref, w_ref, wvt_ref, qkv_ref, vt_ref, z2_ref, *, n_tok):
    tm = x_ref.shape[1]
    h = _norm_mod(x_ref[0], g_ref[...], mod_ref[0, 0:1, :], mod_ref[0, 1:2, :]).astype(BF16)
    q = jnp.dot(h, w_ref[:, 0:ATT_W], preferred_element_type=F32)
    qkv_ref[0, :, 0:ATT_W] = (q * (HEAD_DIM ** -0.5)).astype(BF16)
    kv = jnp.dot(h, w_ref[:, ATT_W:n_tok * ATT_W], preferred_element_type=F32)
    qkv_ref[0, :, ATT_W:n_tok * ATT_W] = kv.astype(BF16)
    vt = lax.dot_general(wvt_ref[...], h, _NT, preferred_element_type=F32).astype(BF16)
    for c in range(tm // LANES):
        vt_ref[0, c] = vt[:, c * LANES:(c + 1) * LANES]
    z2_ref[0] = jnp.dot(h, w_ref[:, 3 * ATT_W:], preferred_element_type=F32)


def _inproj(x, mod, g, w, wvt, tm, n_tok):
    bn, ln, d = x.shape
    nc = tm // LANES
    return pl.pallas_call(
        functools.partial(_inproj_kernel, n_tok=n_tok),
        grid=(bn, ln // tm),
        in_specs=[
            pl.BlockSpec((1, tm, d), lambda b, i: (b, i, 0)),
            pl.BlockSpec((1, 6, d), lambda b, i: (b, 0, 0)),
            _full((1, d)),
            _full((d, N_IN)),
            _full((ATT_W, d)),
        ],
        out_specs=[
            pl.BlockSpec((1, tm, n_tok * ATT_W), lambda b, i: (b, i, 0)),
            pl.BlockSpec((1, nc, ATT_W, LANES), lambda b, i: (b, i, 0, 0)),
            pl.BlockSpec((1, tm, 2 * SG_W + 2 * CV_W), lambda b, i: (b, i, 0)),
        ],
        out_shape=[
            jax.ShapeDtypeStruct((bn, ln, n_tok * ATT_W), BF16),
            jax.ShapeDtypeStruct((bn, ln // LANES, ATT_W, LANES), BF16),
            jax.ShapeDtypeStruct((bn, ln, 2 * SG_W + 2 * CV_W), F32),
        ],
        compiler_params=_cparams(("parallel", "parallel")),
        name="inproj",
    )(x, mod, g, w, wvt)


GRID_ROWS = 2048 // GRID_W
RPB_W = 2 * NA_COLS - 1
RPB_PAD = 512
ROW_CFGS = ((0, 0), (2, 0), (8, 4), (GRID_ROWS - 4, GRID_ROWS - K_ROWS),
            (GRID_ROWS - 2, GRID_ROWS - K_ROWS))


def _table_kernel(rpb_ref, o_ref):
    r_base = pl.program_id(0) * NA_HEADS + pl.program_id(1) * 2
    shape = (GRID_W, LANES)
    kc = lax.broadcasted_iota(jnp.int32, shape, 0)
    lane = lax.broadcasted_iota(jnp.int32, shape, 1)
    low = lane < GRID_W
    qc = jnp.where(low, lane, lane - GRID_W)
    off = kc - qc + (NA_COLS - 1)
    sc = jnp.clip(qc - NA_COLS // 2, 0, GRID_W - NA_COLS)
    in_cols = (kc >= sc) & (kc < sc + NA_COLS)
    neg = jnp.full(shape, NEG_INF, F32)
    for hh in range(2):
        toep = []
        for u in range(2 * NA_ROWS - 1):
            acc = jnp.zeros(shape, F32)
            for w in range(RPB_W):
                acc = jnp.where(off == w, rpb_ref[r_base + hh, u * RPB_W + w], acc)
            toep.append(jnp.where(in_cols, acc, neg))
        for cfg, (r0, a) in enumerate(ROW_CFGS):
            for j in range(K_ROWS):
                kr = a + j
                halves = []
                for qr in (r0, r0 + 1):
                    sr = min(max(qr - NA_ROWS // 2, 0), GRID_ROWS - NA_ROWS)
                    inside = sr <= kr < sr + NA_ROWS
                    halves.append(toep[kr - qr + NA_ROWS - 1] if inside else neg)
                o_ref[0, cfg, 0, j * GRID_W:(j + 1) * GRID_W, hh * LANES:(hh + 1) * LANES] = (
                    jnp.where(low, halves[0], halves[1]))


def _bias_tables(na_rpb):
    assert Q_ROWS == 2
    flat = na_rpb.reshape(DEPTH * NA_HEADS, -1)
    flat = jnp.pad(flat, ((0, 0), (0, RPB_PAD - flat.shape[1])))
    nq2, nk = 2 * Q_ROWS * GRID_W, K_ROWS * GRID_W
    return pl.pallas_call(
        _table_kernel,
        grid=(DEPTH, HEAD_PAIRS),
        in_specs=[pl.BlockSpec(memory_space=pltpu.SMEM)],
        out_specs=pl.BlockSpec((1, len(ROW_CFGS), 1, nk, nq2), lambda l, p: (l, 0, p, 0, 0)),
        out_shape=jax.ShapeDtypeStruct((DEPTH, len(ROW_CFGS), HEAD_PAIRS, nk, nq2), F32),
        compiler_params=_cparams(("arbitrary", "arbitrary")),
        name="bias_tables",
    )(flat)


def _stack_heads(q2):
    lane = lax.broadcasted_iota(jnp.int32, q2.shape, 1)
    zero = jnp.zeros_like(q2)
    return jnp.concatenate([jnp.where(lane < HEAD_DIM, q2, zero),
                            jnp.where(lane < HEAD_DIM, zero, q2)], axis=0)


def _unstack_heads(o, n):
    lane = lax.broadcasted_iota(jnp.int32, (n, LANES), 1)
    return jnp.where(lane < HEAD_DIM, o[0:n], o[n:2 * n])


def _attn_kernel(q_ref, k_ref, vt_ref, kc_ref, vct_ref, tab_ref, o_ref):
    nq = Q_ROWS * GRID_W
    nrb = q_ref.shape[1] // nq
    rows = q_ref.shape[1] // GRID_W
    kchunks = K_ROWS * GRID_W // LANES
    row = lax.broadcasted_iota(jnp.int32, (LANES, nq), 0)

    def body(it, carry):
        def place(u):
            rb = it * ATT_UNROLL + u
            c0 = jnp.clip(rb - NA_ROWS // 4, 0, (rows - K_ROWS) // 2)
            cfg = jnp.where(rb < 2, rb, jnp.where(rb < nrb - 2, 2, rb - (nrb - 5)))
            return c0, cfg, pl.multiple_of(rb * nq, nq), pl.multiple_of(c0 * LANES, LANES)

        places = [place(u) for u in range(ATT_UNROLL)]

        def scores(u, p):
            _, cfg, q0, k0 = places[u]
            cols = slice(p * LANES, (p + 1) * LANES)
            qs = _stack_heads(q_ref[0, pl.ds(q0, nq), cols])
            s_loc = lax.dot_general(k_ref[0, pl.ds(k0, K_ROWS * GRID_W), cols], qs, _NT,
                                    preferred_element_type=F32) + tab_ref[0, cfg, p]
            s_ctx = lax.dot_general(kc_ref[0, :, cols], qs, _NT, preferred_element_type=F32)
            return s_loc, s_ctx

        steps = [(u, p) for u in range(ATT_UNROLL) for p in range(HEAD_PAIRS)]
        nxt = scores(*steps[0])
        for i, (u, p) in enumerate(steps):
            c0, _, q0, _ = places[u]
            cols = slice(p * LANES, (p + 1) * LANES)
            s_loc, s_ctx = nxt
            if i + 1 < len(steps):
                nxt = scores(*steps[i + 1])
            m = jnp.maximum(jnp.max(s_loc, axis=0, keepdims=True),
                            jnp.max(s_ctx, axis=0, keepdims=True))
            p_loc = jnp.exp(s_loc - m)
            p_ctx = jnp.exp(s_ctx - m)
            den = jnp.sum(p_loc, axis=0, keepdims=True) + jnp.sum(p_ctx, axis=0, keepdims=True)
            vt = jnp.concatenate([vt_ref[0, c0 + c, cols, :] for c in range(kchunks)], axis=1)
            vct = jnp.concatenate([vct_ref[0, c, cols, :] for c in range(vct_ref.shape[1])],
                                  axis=1)
            o = jnp.dot(vt, p_loc.astype(BF16), preferred_element_type=F32)
            o = (o + jnp.dot(vct, p_ctx.astype(BF16), preferred_element_type=F32)) / den
            o2 = jnp.where(row < HEAD_DIM, o[:, 0:nq], o[:, nq:2 * nq])
            o_ref[0, pl.ds(q0, nq), cols] = o2.T.astype(BF16)
        return carry

    lax.fori_loop(0, nrb // ATT_UNROLL, body, 0)


def _attention(qk, vt, qkv_c, vt_c, tables, layer):
    bn, ln, _ = qk.shape
    lc = qkv_c.shape[1]
    return pl.pallas_call(
        _attn_kernel,
        grid=(bn,),
        in_specs=[
            pl.BlockSpec((1, ln, ATT_W), lambda b: (b, 0, 0)),
            pl.BlockSpec((1, ln, ATT_W), lambda b: (b, 0, 1)),
            pl.BlockSpec((1,) + vt.shape[1:], lambda b: (b, 0, 0, 0)),
            pl.BlockSpec((1, lc, ATT_W), lambda b: (b, 0, 1)),
            pl.BlockSpec((1,) + vt_c.shape[1:], lambda b: (b, 0, 0, 0)),
            pl.BlockSpec((1,) + tables.shape[1:], lambda b: (layer, 0, 0, 0, 0),
                         pipeline_mode=pl.Buffered(1)),
        ],
        out_specs=pl.BlockSpec((1, ln, ATT_W), lambda b: (b, 0, 0)),
        out_shape=jax.ShapeDtypeStruct((bn, ln, ATT_W), BF16),
        compiler_params=_cparams(("parallel",)),
        name="attention",
    )(qk, qk, vt, qkv_c, vt_c, tables)


def _ctx_attn_kernel(q_ref, k_ref, v_ref, o_ref):
    n = q_ref.shape[1]
    for p in range(HEAD_PAIRS):
        cols = slice(p * LANES, (p + 1) * LANES)
        qs = _stack_heads(q_ref[0, :, cols])
        s = lax.dot_general(qs, k_ref[0, :, cols], _NT, preferred_element_type=F32)
        e = jnp.exp(s - jnp.max(s, axis=-1, keepdims=True))
        den = jnp.sum(e, axis=-1, keepdims=True)
        o = jnp.dot(e.astype(BF16), v_ref[0, :, cols], preferred_element_type=F32)
        o_ref[0, :, cols] = _unstack_heads(o / den, n).astype(BF16)


def _ctx_attention(qkv_c):
    bn, lc, _ = qkv_c.shape
    return pl.pallas_call(
        _ctx_attn_kernel,
        grid=(bn,),
        in_specs=[pl.BlockSpec((1, lc, ATT_W), lambda b, j=j: (b, 0, j)) for j in range(3)],
        out_specs=pl.BlockSpec((1, lc, ATT_W), lambda b: (b, 0, 0)),
        out_shape=jax.ShapeDtypeStruct((bn, lc, ATT_W), BF16),
        compiler_params=_cparams(("parallel",)),
        name="ctx_attention",
    )(qkv_c, qkv_c, qkv_c)


def _mix_kernel(att_ref, z_ref, zp_ref, zn_ref, x_ref, mod_ref, sgg_ref, sgb_ref, ws_ref,
                sbias_ref, cvw_ref, cvb_ref, cvg_ref, cvnb_ref, gmat_ref, wout_ref,
                o_ref):
    tm = x_ref.shape[1]
    i = pl.program_id(1)
    nt = pl.num_programs(1)

    u = jax.nn.gelu(z_ref[0, :, 0:SG_W])
    v = jax.nn.gelu(z_ref[0, :, SG_W:2 * SG_W])
    mu = jnp.mean(v, axis=-1, keepdims=True)
    vc = v - mu
    var = jnp.mean(vc * vc, axis=-1, keepdims=True)
    vn = ((vc * lax.rsqrt(var + EPS)) * sgg_ref[...] + sgb_ref[...]).astype(BF16)
    lane = lax.broadcasted_iota(jnp.int32, (SG_CHUNK, SG_W), 1)
    gw = SG_W // SG_GROUPS
    mixed = []
    for c in range(tm // SG_CHUNK):
        r = jnp.dot(ws_ref[...], vn[c * SG_CHUNK:(c + 1) * SG_CHUNK], preferred_element_type=F32)
        m = r[0:SG_CHUNK]
        for g in range(1, SG_GROUPS):
            m = jnp.where(lane >= g * gw, r[g * SG_CHUNK:(g + 1) * SG_CHUNK], m)
        mixed.append(m + sbias_ref[...])
    sg = u * jnp.concatenate(mixed, axis=0)

    a0 = 2 * SG_W
    g0 = a0 + CV_W
    hm = z_ref[0, :, a0:g0] * jax.nn.sigmoid(z_ref[0, :, g0:g0 + CV_W])
    hp = zp_ref[0, :, 0:CV_W] * jax.nn.sigmoid(zp_ref[0, :, CV_W:2 * CV_W])
    hn = zn_ref[0, :, 0:CV_W] * jax.nn.sigmoid(zn_ref[0, :, CV_W:2 * CV_W])
    hf = jnp.concatenate([jnp.where(i > 0, hp, 0.0), hm, jnp.where(i < nt - 1, hn, 0.0)], axis=0)
    n = tm + 2 * CV_HALO
    shifted = [hf] + [pltpu.roll(hf, n - b, axis=0) for b in range(1, 8)]
    y = jnp.zeros((tm, CV_W), F32) + cvb_ref[...]
    for k in range(CV_KERNEL):
        s = CV_HALO - CV_KERNEL // 2 + k
        y = y + cvw_ref[k:k + 1, :] * shifted[s % 8][8 * (s // 8):8 * (s // 8) + tm]

    def group_mean(t):
        hi = t.astype(BF16)
        lo = (t - hi.astype(F32)).astype(BF16)
        return (jnp.dot(hi, gmat_ref[...], preferred_element_type=F32)
                + jnp.dot(lo, gmat_ref[...], preferred_element_type=F32))

    yc = y - group_mean(y)
    yn = yc * lax.rsqrt(group_mean(yc * yc) + EPS) * cvg_ref[...] + cvnb_ref[...]
    cv = yn * jax.nn.sigmoid(yn)

    out = jnp.dot(att_ref[0], wout_ref[0:ATT_W, :], preferred_element_type=F32)
    out = out + jnp.dot(sg.astype(BF16), wout_ref[ATT_W:ATT_W + SG_W, :],
                        preferred_element_type=F32)
    out = out + jnp.dot(cv.astype(BF16), wout_ref[ATT_W + SG_W:, :], preferred_element_type=F32)
    o_ref[0] = x_ref[0] + mod_ref[0, 2:3, :] * out


def _mix(att, z2, x, mod, sgg, sgb, ws, sbias, cvw, cvb, cvg, cvnb, gmat, wout, tm):
    bn, ln, d = x.shape
    nh = tm // CV_HALO
    last = ln // CV_HALO - 1
    zw = 2 * SG_W + 2 * CV_W
    return pl.pallas_call(
        _mix_kernel,
        grid=(bn, ln // tm),
        in_specs=[
            pl.BlockSpec((1, tm, ATT_W), lambda b, i: (b, i, 0)),
            pl.BlockSpec((1, tm, zw), lambda b, i: (b, i, 0)),
            pl.BlockSpec((1, CV_HALO, 2 * CV_W), lambda b, i: (b, jnp.maximum(i * nh - 1, 0), 1)),
            pl.BlockSpec((1, CV_HALO, 2 * CV_W),
                         lambda b, i: (b, jnp.minimum((i + 1) * nh, last), 1)),
            pl.BlockSpec((1, tm, d), lambda b, i: (b, i, 0)),
            pl.BlockSpec((1, 6, d), lambda b, i: (b, 0, 0)),
            _full((1, SG_W)), _full((1, SG_W)),
            _full((SG_GROUPS * SG_CHUNK, SG_CHUNK)),
            _full((SG_CHUNK, SG_W)),
            _full((CV_KERNEL, CV_W)), _full((1, CV_W)), _full((1, CV_W)), _full((1, CV_W)),
            _full((CV_W, CV_W)),
            _full((d, d)),
        ],
        out_specs=pl.BlockSpec((1, tm, d), lambda b, i: (b, i, 0)),
        out_shape=jax.ShapeDtypeStruct((bn, ln, d), F32),
        compiler_params=_cparams(("parallel", "parallel")),
        name="mix_out",
    )(att, z2, z2, z2, x, mod, sgg, sgb, ws, sbias, cvw, cvb, cvg, cvnb, gmat, wout)


def _ffn_kernel(x_ref, xp_ref, xn_ref, mod_ref, g_ref, wup_ref, cw_ref, cb_ref, wdn_ref, fg_ref,
                o_ref, act_scr, *, final):
    tm = x_ref.shape[1]
    i = pl.program_id(1)
    nt = pl.num_programs(1)
    shift, scale = mod_ref[0, 3:4, :], mod_ref[0, 4:5, :]
    xm = x_ref[0]
    hm = _norm_mod(xm, g_ref[...], shift, scale).astype(BF16)
    hp = jnp.where(i > 0, _norm_mod(xp_ref[0], g_ref[...], shift, scale), 0.0).astype(BF16)
    hn = jnp.where(i < nt - 1, _norm_mod(xn_ref[0], g_ref[...], shift, scale), 0.0).astype(BF16)
    hh = jnp.concatenate([hp, hm, hn], axis=0)

    n = tm + 2 * FFN_HALO
    for j in range(D_FF // FF_CHUNK):
        cols = slice(j * FF_CHUNK, (j + 1) * FF_CHUNK)
        g = jnp.dot(hh, wup_ref[:, cols], preferred_element_type=F32)
        up = jnp.dot(hm, wup_ref[:, D_FF + j * FF_CHUNK:D_FF + (j + 1) * FF_CHUNK],
                     preferred_element_type=F32)
        g_prev = pltpu.roll(g, 1, axis=0)[FFN_HALO:FFN_HALO + tm]
        g_next = pltpu.roll(g, n - 1, axis=0)[FFN_HALO:FFN_HALO + tm]
        gc = (cb_ref[:, cols] + cw_ref[0:1, cols] * g_prev
              + cw_ref[1:2, cols] * g[FFN_HALO:FFN_HALO + tm] + cw_ref[2:3, cols] * g_next)
        act_scr[:, cols] = (gc * jax.nn.sigmoid(gc) * up).astype(BF16)
    acc = jnp.dot(act_scr[...], wdn_ref[...], preferred_element_type=F32)
    xo = xm + mod_ref[0, 5:6, :] * acc
    if final:
        ms = jnp.mean(xo * xo, axis=-1, keepdims=True)
        xo = xo * lax.rsqrt(ms + EPS) * fg_ref[...]
    o_ref[0] = xo


def _ffn(x, mod, g, wup, cw, cb, wdn, fg, tm, final):
    bn, ln, d = x.shape
    nh = tm // FFN_HALO
    last = ln // FFN_HALO - 1
    single = dict(pipeline_mode=pl.Buffered(1))
    return pl.pallas_call(
        functools.partial(_ffn_kernel, final=final),
        grid=(bn, ln // tm),
        in_specs=[
            pl.BlockSpec((1, tm, d), lambda b, i: (b, i, 0)),
            pl.BlockSpec((1, FFN_HALO, d), lambda b, i: (b, jnp.maximum(i * nh - 1, 0), 0)),
            pl.BlockSpec((1, FFN_HALO, d), lambda b, i: (b, jnp.minimum((i + 1) * nh, last), 0)),
            pl.BlockSpec((1, 6, d), lambda b, i: (b, 0, 0)),
            _full((1, d)),
            pl.BlockSpec((d, 2 * D_FF), lambda b, i: (0, 0), **single),
            _full((FFN_KERNEL, D_FF)), _full((1, D_FF)),
            pl.BlockSpec((D_FF, d), lambda b, i: (0, 0), **single),
            _full((1, d)),
        ],
        out_specs=pl.BlockSpec((1, tm, d), lambda b, i: (b, i, 0)),
        out_shape=jax.ShapeDtypeStruct((bn, ln, d), F32),
        scratch_shapes=[pltpu.VMEM((tm, D_FF), BF16)],
        compiler_params=_cparams(("parallel", "parallel")),
        name="conv_ffn",
    )(x, x, x, mod, g, wup, cw, cb, wdn, fg)


def kernel(x, c, ctx, c_ctx, ada_w, ada_b, norm1_g, w_in, na_rpb, sg_norm_g, sg_norm_b, sg_w,
           sg_b, cv_w, cv_b, cv_norm_g, cv_norm_b, w_out, norm2_g, ffn_w_up, ffn_conv_w,
           ffn_conv_b, ffn_w_down, final_norm_g):
    bn, ln, d = x.shape
    lc = ctx.shape[1]
    assert (d, ln, ln // GRID_W) == (D_MODEL, 2048, 32) and lc % SG_CHUNK == 0

    rows = -(-(bn + 1) // 8) * 8
    c_all = jnp.zeros((rows, d), F32).at[:bn].set(c).at[bn].set(c_ctx)
    mods = _ada_table(c_all, ada_w, ada_b)
    tables = _bias_tables(na_rpb)

    w_in_b = w_in.astype(BF16)
    wvt_b = w_in_b[:, :, 2 * ATT_W:3 * ATT_W].transpose(0, 2, 1)
    w_out_b = w_out.astype(BF16)
    w_up_b = ffn_w_up.astype(BF16)
    w_dn_b = ffn_w_down.astype(BF16)
    ws_b = sg_w.reshape(DEPTH, SG_GROUPS * SG_CHUNK, SG_CHUNK).astype(BF16)
    sbias = jnp.repeat(sg_b.transpose(0, 2, 1), SG_W // SG_GROUPS, axis=2)
    grp = np.arange(CV_W) // (CV_W // CV_GROUPS)
    gmat = jnp.asarray((grp[:, None] == grp[None, :]) / (CV_W // CV_GROUPS), BF16)
    fg = final_norm_g.reshape(1, d)

    tm, tmc = 512, lc
    xc = ctx
    for l in range(DEPTH):
        last = l == DEPTH - 1
        mod_l = mods[l, :bn].reshape(bn, 6, d)
        mod_c = jnp.broadcast_to(mods[l, bn].reshape(1, 6, d), (bn, 6, d))
        g1 = norm1_g[l].reshape(1, d)
        g2 = norm2_g[l].reshape(1, d)
        mix_w = (sg_norm_g[l].reshape(1, SG_W), sg_norm_b[l].reshape(1, SG_W), ws_b[l], sbias[l],
                 cv_w[l], cv_b[l].reshape(1, CV_W), cv_norm_g[l].reshape(1, CV_W),
                 cv_norm_b[l].reshape(1, CV_W), gmat, w_out_b[l])
        ffn_w = (w_up_b[l], ffn_conv_w[l], ffn_conv_b[l].reshape(1, D_FF), w_dn_b[l], fg)

        qkv_c, vt_c, z2_c = _inproj(xc, mod_c, g1, w_in_b[l], wvt_b[l], tmc, 3)
        qk, vt, z2 = _inproj(x, mod_l, g1, w_in_b[l], wvt_b[l], tm, 2)
        att = _attention(qk, vt, qkv_c, vt_c, tables, l)
        x = _mix(att, z2, x, mod_l, *mix_w, tm)
        x = _ffn(x, mod_l, g2, *ffn_w, tm, last)
        if not last:
            att_c = _ctx_attention(qkv_c)
            xc = _mix(att_c, z2_c, xc, mod_c, *mix_w, tmc)
            xc = _ffn(xc, mod_c, g2, *ffn_w, tmc, False)
    return x
```

```python
import functools

import jax
import jax.numpy as jnp
import numpy as np
from jax import lax
from jax.experimental import pallas as pl
from jax.experimental.pallas import tpu as pltpu

D_MODEL = 1024
DEPTH = 4
GRID_W = 64
HEAD_DIM = 64
ATT_W = 512
NA_HEADS = 8
NA_ROWS = 8
NA_COLS = 16
SG_W = 256
SG_GROUPS = 4
SG_CHUNK = 128
CV_W = 256
CV_GROUPS = 4
CV_KERNEL = 31
N_IN = 3 * ATT_W + 2 * SG_W + 2 * CV_W
D_FF = 2816
FFN_KERNEL = 3
EPS = 1e-6
NEG_INF = -1e30

LANES = 128
HEAD_PAIRS = ATT_W // LANES
Q_ROWS = 2
K_ROWS = 10
CV_HALO = 16
FFN_HALO = 8
FF_CHUNK = 256
ATT_UNROLL = 8
ATT_AHEAD = 2
LOG2E = 1.4426950408889634
SUM_ROWS = 16
VMEM_LIMIT = 56 * 1024 * 1024

F32 = jnp.float32
BF16 = jnp.bfloat16


def _cparams(sem):
    return pltpu.CompilerParams(dimension_semantics=sem, vmem_limit_bytes=VMEM_LIMIT)


def _full(shape):
    nd = len(shape)
    return pl.BlockSpec(shape, lambda *_: (0,) * nd)


def _layer(shape, layer, **kw):
    nd = len(shape)
    return pl.BlockSpec((None,) + tuple(shape), lambda *_: (layer,) + (0,) * nd, **kw)


def _ada_kernel(c_ref, w_ref, b_ref, o_ref):
    cs = c_ref[...]
    cs = cs * jax.nn.sigmoid(cs)
    o_ref[0] = jnp.dot(cs.astype(BF16), w_ref[0].astype(BF16),
                       preferred_element_type=F32) + b_ref[0]


def _ada_table(c_all, ada_w, ada_b):
    rows = c_all.shape[0]
    tn = 1536
    return pl.pallas_call(
        _ada_kernel,
        grid=(DEPTH, 6 * D_MODEL // tn),
        in_specs=[
            pl.BlockSpec((rows, D_MODEL), lambda l, j: (0, 0)),
            pl.BlockSpec((1, D_MODEL, tn), lambda l, j: (l, 0, j)),
            pl.BlockSpec((1, 1, tn), lambda l, j: (l, 0, j)),
        ],
        out_specs=pl.BlockSpec((1, rows, tn), lambda l, j: (l, 0, j)),
        out_shape=jax.ShapeDtypeStruct((DEPTH, rows, 6 * D_MODEL), F32),
        compiler_params=_cparams(("arbitrary", "arbitrary")),
        name="ada_table",
    )(c_all, ada_w, ada_b.reshape(DEPTH, 1, 6 * D_MODEL))


def _norm_mod(x, g, shift, scale):
    ms = jnp.mean(x * x, axis=-1, keepdims=True)
    return (x * lax.rsqrt(ms + EPS) * g) * (1.0 + scale) + shift


_NT = (((1,), (1,)), ((), ()))


def _inproj_kernel(x_ref, mod_ref, g_ref, w_ref, wvt_ref, qkv_ref, vt_ref, z2_ref, *, n_tok):
    tm = x_ref.shape[1]
    h = _norm_mod(x_ref[0], g_ref[...], mod_ref[0, 0:1, :], mod_ref[0, 1:2, :]).astype(BF16)
    q = jnp.dot(h, w_ref[:, 0:ATT_W], preferred_element_type=F32)
    qkv_ref[0, :, 0:ATT_W] = (q * (HEAD_DIM ** -0.5 * LOG2E)).astype(BF16)
    kv = jnp.dot(h, w_ref[:, ATT_W:n_tok * ATT_W], preferred_element_type=F32)
    qkv_ref[0, :, ATT_W:n_tok * ATT_W] = kv.astype(BF16)
    vt = lax.dot_general(wvt_ref[...], h, _NT, preferred_element_type=F32).astype(BF16)
    for c in range(tm // LANES):
        vt_ref[0, c] = vt[:, c * LANES:(c + 1) * LANES]
    z2_ref[0] = jnp.dot(h, w_ref[:, 3 * ATT_W:], preferred_element_type=F32)


def _inproj(x, mod, g, w_all, wvt_all, layer, tm, n_tok):
    bn, ln, d = x.shape
    nc = tm // LANES
    return pl.pallas_call(
        functools.partial(_inproj_kernel, n_tok=n_tok),
        grid=(bn, ln // tm),
        in_specs=[
            pl.BlockSpec((1, tm, d), lambda b, i: (b, i, 0)),
            pl.BlockSpec((1, 6, d), lambda b, i: (b, 0, 0)),
            _full((1, d)),
            _layer((d, N_IN), layer),
            _layer((ATT_W, d), layer),
        ],
        out_specs=[
            pl.BlockSpec((1, tm, n_tok * ATT_W), lambda b, i: (b, i, 0)),
            pl.BlockSpec((1, nc, ATT_W, LANES), lambda b, i: (b, i, 0, 0)),
            pl.BlockSpec((1, tm, 2 * SG_W + 2 * CV_W), lambda b, i: (b, i, 0)),
        ],
        out_shape=[
            jax.ShapeDtypeStruct((bn, ln, n_tok * ATT_W), BF16),
            jax.ShapeDtypeStruct((bn, ln // LANES, ATT_W, LANES), BF16),
            jax.ShapeDtypeStruct((bn, ln, 2 * SG_W + 2 * CV_W), F32),
        ],
        compiler_params=_cparams(("parallel", "parallel")),
        name="inproj",
    )(x, mod, g, w_all, wvt_all)


GRID_ROWS = 2048 // GRID_W
RPB_W = 2 * NA_COLS - 1
RPB_PAD = 512
ROW_CFGS = ((0, 0), (2, 0), (8, 4), (GRID_ROWS - 4, GRID_ROWS - K_ROWS),
            (GRID_ROWS - 2, GRID_ROWS - K_ROWS))


def _table_kernel(rpb_ref, o_ref):
    r_base = pl.program_id(0) * NA_HEADS + pl.program_id(1) * 2
    shape = (GRID_W, LANES)
    kc = lax.broadcasted_iota(jnp.int32, shape, 0)
    lane = lax.broadcasted_iota(jnp.int32, shape, 1)
    low = lane < GRID_W
    qc = jnp.where(low, lane, lane - GRID_W)
    off = kc - qc + (NA_COLS - 1)
    sc = jnp.clip(qc - NA_COLS // 2, 0, GRID_W - NA_COLS)
    in_cols = (kc >= sc) & (kc < sc + NA_COLS)
    neg = jnp.full(shape, NEG_INF, F32)
    for hh in range(2):
        toep = []
        for u in range(2 * NA_ROWS - 1):
            acc = jnp.zeros(shape, F32)
            for w in range(RPB_W):
                acc = jnp.where(off == w, rpb_ref[r_base + hh, u * RPB_W + w], acc)
            toep.append(jnp.where(in_cols, acc * LOG2E, neg))
        for cfg, (r0, a) in enumerate(ROW_CFGS):
            for j in range(K_ROWS):
                kr = a + j
                halves = []
                for qr in (r0, r0 + 1):
                    sr = min(max(qr - NA_ROWS // 2, 0), GRID_ROWS - NA_ROWS)
                    inside = sr <= kr < sr + NA_ROWS
                    halves.append(toep[kr - qr + NA_ROWS - 1] if inside else neg)
                o_ref[0, cfg, 0, j * GRID_W:(j + 1) * GRID_W, hh * LANES:(hh + 1) * LANES] = (
                    jnp.where(low, halves[0], halves[1]))


def _bias_tables(na_rpb):
    assert Q_ROWS == 2
    flat = na_rpb.reshape(DEPTH * NA_HEADS, -1)
    flat = jnp.pad(flat, ((0, 0), (0, RPB_PAD - flat.shape[1])))
    nq2, nk = 2 * Q_ROWS * GRID_W, K_ROWS * GRID_W
    return pl.pallas_call(
        _table_kernel,
        grid=(DEPTH, HEAD_PAIRS),
        in_specs=[pl.BlockSpec(memory_space=pltpu.SMEM)],
        out_specs=pl.BlockSpec((1, len(ROW_CFGS), 1, nk, nq2), lambda l, p: (l, 0, p, 0, 0)),
        out_shape=jax.ShapeDtypeStruct((DEPTH, len(ROW_CFGS), HEAD_PAIRS, nk, nq2), F32),
        compiler_params=_cparams(("arbitrary", "arbitrary")),
        name="bias_tables",
    )(flat)


def _stack_heads(q2):
    lane = lax.broadcasted_iota(jnp.int32, q2.shape, 1)
    zero = jnp.zeros_like(q2)
    return jnp.concatenate([jnp.where(lane < HEAD_DIM, q2, zero),
                            jnp.where(lane < HEAD_DIM, zero, q2)], axis=0)


def _unstack_heads(o, n):
    lane = lax.broadcasted_iota(jnp.int32, (n, LANES), 1)
    return jnp.where(lane < HEAD_DIM, o[0:n], o[n:2 * n])


def _attn_kernel(q_ref, k_ref, vt_ref, kc_ref, vct_ref, tab_ref, o_ref):
    nq = Q_ROWS * GRID_W
    nk = K_ROWS * GRID_W
    nrb = q_ref.shape[1] // nq
    rows = q_ref.shape[1] // GRID_W
    row = lax.broadcasted_iota(jnp.int32, (LANES, nq), 0)
    ones_loc = jnp.ones((SUM_ROWS, nk), BF16)
    ones_ctx = jnp.ones((SUM_ROWS, kc_ref.shape[1]), BF16)

    def body(it, carry):
        def place(u):
            rb = it * ATT_UNROLL + u
            c0 = jnp.clip(rb - NA_ROWS // 4, 0, (rows - K_ROWS) // 2)
            cfg = jnp.where(rb < 2, rb, jnp.where(rb < nrb - 2, 2, rb - (nrb - 5)))
            return c0, cfg, pl.multiple_of(rb * nq, nq), pl.multiple_of(c0 * LANES, LANES)

        places = [place(u) for u in range(ATT_UNROLL)]

        def scores(u, p):
            _, cfg, q0, k0 = places[u]
            cols = slice(p * LANES, (p + 1) * LANES)
            qs = _stack_heads(q_ref[0, pl.ds(q0, nq), cols])
            s_loc = lax.dot_general(k_ref[0, pl.ds(k0, nk), cols], qs, _NT,
                                    preferred_element_type=F32) + tab_ref[0, cfg, p]
            s_ctx = lax.dot_general(kc_ref[0, :, cols], qs, _NT, preferred_element_type=F32)
            return s_loc, s_ctx

        steps = [(u, p) for u in range(ATT_UNROLL) for p in range(HEAD_PAIRS)]
        ahead = [scores(*st) for st in steps[:ATT_AHEAD]]
        for i, (u, p) in enumerate(steps):
            c0, _, q0, _ = places[u]
            cols = slice(p * LANES, (p + 1) * LANES)
            s_loc, s_ctx = ahead.pop(0)
            if i + ATT_AHEAD < len(steps):
                ahead.append(scores(*steps[i + ATT_AHEAD]))
            m = jnp.maximum(jnp.max(s_loc, axis=0, keepdims=True),
                            jnp.max(s_ctx, axis=0, keepdims=True))
            p_loc = jnp.exp2(s_loc - m).astype(BF16)
            p_ctx = jnp.exp2(s_ctx - m).astype(BF16)
            vt = jnp.concatenate([vt_ref[0, c0 + c, cols, :] for c in range(nk // LANES)]
                                 , axis=1)
            vct = jnp.concatenate([vct_ref[0, c, cols, :] for c in range(vct_ref.shape[1])],
                                  axis=1)
            o = (jnp.dot(jnp.concatenate([vt, ones_loc], axis=0), p_loc,
                         preferred_element_type=F32)
                 + jnp.dot(jnp.concatenate([vct, ones_ctx], axis=0), p_ctx,
                           preferred_element_type=F32))
            o = o[0:LANES] / o[LANES:LANES + 1]
            o2 = jnp.where(row < HEAD_DIM, o[:, 0:nq], o[:, nq:2 * nq])
            o_ref[0, pl.ds(q0, nq), cols] = o2.T.astype(BF16)
        return carry

    lax.fori_loop(0, nrb // ATT_UNROLL, body, 0)


def _attention(qk, vt, qkv_c, vt_c, tables, layer):
    bn, ln, _ = qk.shape
    lc = qkv_c.shape[1]
    return pl.pallas_call(
        _attn_kernel,
        grid=(bn,),
        in_specs=[
            pl.BlockSpec((1, ln, ATT_W), lambda b: (b, 0, 0)),
            pl.BlockSpec((1, ln, ATT_W), lambda b: (b, 0, 1)),
            pl.BlockSpec((1,) + vt.shape[1:], lambda b: (b, 0, 0, 0)),
            pl.BlockSpec((1, lc, ATT_W), lambda b: (b, 0, 1)),
            pl.BlockSpec((1,) + vt_c.shape[1:], lambda b: (b, 0, 0, 0)),
            pl.BlockSpec((1,) + tables.shape[1:], lambda b: (layer, 0, 0, 0, 0),
                         pipeline_mode=pl.Buffered(1)),
        ],
        out_specs=pl.BlockSpec((1, ln, ATT_W), lambda b: (b, 0, 0)),
        out_shape=jax.ShapeDtypeStruct((bn, ln, ATT_W), BF16),
        compiler_params=_cparams(("parallel",)),
        name="attention",
    )(qk, qk, vt, qkv_c, vt_c, tables)


def _ctx_attn_kernel(q_ref, k_ref, v_ref, o_ref):
    n = q_ref.shape[1]
    for p in range(HEAD_PAIRS):
        cols = slice(p * LANES, (p + 1) * LANES)
        qs = _stack_heads(q_ref[0, :, cols])
        s = lax.dot_general(qs, k_ref[0, :, cols], _NT, preferred_element_type=F32)
        e = jnp.exp2(s - jnp.max(s, axis=-1, keepdims=True))
        den = jnp.sum(e, axis=-1, keepdims=True)
        o = jnp.dot(e.astype(BF16), v_ref[0, :, cols], preferred_element_type=F32)
        o_ref[0, :, cols] = _unstack_heads(o / den, n).astype(BF16)


def _ctx_attention(qkv_c):
    bn, lc, _ = qkv_c.shape
    return pl.pallas_call(
        _ctx_attn_kernel,
        grid=(bn,),
        in_specs=[pl.BlockSpec((1, lc, ATT_W), lambda b, j=j: (b, 0, j)) for j in range(3)],
        out_specs=pl.BlockSpec((1, lc, ATT_W), lambda b: (b, 0, 0)),
        out_shape=jax.ShapeDtypeStruct((bn, lc, ATT_W), BF16),
        compiler_params=_cparams(("parallel",)),
        name="ctx_attention",
    )(qkv_c, qkv_c, qkv_c)


def _mix_kernel(att_ref, z_ref, zp_ref, zn_ref, x_ref, mod_ref, sgg_ref, sgb_ref, ws_ref,
                sbias_ref, cvw_ref, cvb_ref, cvg_ref, cvnb_ref, gmat_ref, wout_ref,
                o_ref):
    tm = x_ref.shape[1]
    i = pl.program_id(1)
    nt = pl.num_programs(1)

    u = jax.nn.gelu(z_ref[0, :, 0:SG_W])
    v = jax.nn.gelu(z_ref[0, :, SG_W:2 * SG_W])
    mu = jnp.mean(v, axis=-1, keepdims=True)
    vc = v - mu
    var = jnp.mean(vc * vc, axis=-1, keepdims=True)
    vn = ((vc * lax.rsqrt(var + EPS)) * sgg_ref[...] + sgb_ref[...]).astype(BF16)
    lane = lax.broadcasted_iota(jnp.int32, (SG_CHUNK, SG_W), 1)
    gw = SG_W // SG_GROUPS
    mixed = []
    for c in range(tm // SG_CHUNK):
        r = jnp.dot(ws_ref[...], vn[c * SG_CHUNK:(c + 1) * SG_CHUNK], preferred_element_type=F32)
        m = r[0:SG_CHUNK]
        for g in range(1, SG_GROUPS):
            m = jnp.where(lane >= g * gw, r[g * SG_CHUNK:(g + 1) * SG_CHUNK], m)
        mixed.append(m + sbias_ref[...])
    sg = u * jnp.concatenate(mixed, axis=0)

    a0 = 2 * SG_W
    g0 = a0 + CV_W
    hm = z_ref[0, :, a0:g0] * jax.nn.sigmoid(z_ref[0, :, g0:g0 + CV_W])
    hp = zp_ref[0, :, 0:CV_W] * jax.nn.sigmoid(zp_ref[0, :, CV_W:2 * CV_W])
    hn = zn_ref[0, :, 0:CV_W] * jax.nn.sigmoid(zn_ref[0, :, CV_W:2 * CV_W])
    hf = jnp.concatenate([jnp.where(i > 0, hp, 0.0), hm, jnp.where(i < nt - 1, hn, 0.0)], axis=0)
    n = tm + 2 * CV_HALO
    shifted = [hf] + [pltpu.roll(hf, n - b, axis=0) for b in range(1, 8)]
    y = jnp.zeros((tm, CV_W), F32) + cvb_ref[...]
    for k in range(CV_KERNEL):
        s = CV_HALO - CV_KERNEL // 2 + k
        y = y + cvw_ref[k:k + 1, :] * shifted[s % 8][8 * (s // 8):8 * (s // 8) + tm]

    def group_mean(t):
        hi = t.astype(BF16)
        lo = (t - hi.astype(F32)).astype(BF16)
        return (jnp.dot(hi, gmat_ref[...], preferred_element_type=F32)
                + jnp.dot(lo, gmat_ref[...], preferred_element_type=F32))

    yc = y - group_mean(y)
    yn = yc * lax.rsqrt(group_mean(yc * yc) + EPS) * cvg_ref[...] + cvnb_ref[...]
    cv = yn * jax.nn.sigmoid(yn)

    out = jnp.dot(att_ref[0], wout_ref[0:ATT_W, :], preferred_element_type=F32)
    out = out + jnp.dot(sg.astype(BF16), wout_ref[ATT_W:ATT_W + SG_W, :],
                        preferred_element_type=F32)
    out = out + jnp.dot(cv.astype(BF16), wout_ref[ATT_W + SG_W:, :], preferred_element_type=F32)
    o_ref[0] = x_ref[0] + mod_ref[0, 2:3, :] * out


def _mix(att, z2, x, mod, sgg, sgb, ws, sbias, cvw, cvb, cvg, cvnb, gmat, wout_all, layer, tm):
    bn, ln, d = x.shape
    nh = tm // CV_HALO
    last = ln // CV_HALO - 1
    zw = 2 * SG_W + 2 * CV_W
    return pl.pallas_call(
        _mix_kernel,
        grid=(bn, ln // tm),
        in_specs=[
            pl.BlockSpec((1, tm, ATT_W), lambda b, i: (b, i, 0)),
            pl.BlockSpec((1, tm, zw), lambda b, i: (b, i, 0)),
            pl.BlockSpec((1, CV_HALO, 2 * CV_W), lambda b, i: (b, jnp.maximum(i * nh - 1, 0), 1)),
            pl.BlockSpec((1, CV_HALO, 2 * CV_W),
                         lambda b, i: (b, jnp.minimum((i + 1) * nh, last), 1)),
            pl.BlockSpec((1, tm, d), lambda b, i: (b, i, 0)),
            pl.BlockSpec((1, 6, d), lambda b, i: (b, 0, 0)),
            _full((1, SG_W)), _full((1, SG_W)),
            _full((SG_GROUPS * SG_CHUNK, SG_CHUNK)),
            _full((SG_CHUNK, SG_W)),
            _full((CV_KERNEL, CV_W)), _full((1, CV_W)), _full((1, CV_W)), _full((1, CV_W)),
            _full((CV_W, CV_W)),
            _layer((d, d), layer),
        ],
        out_specs=pl.BlockSpec((1, tm, d), lambda b, i: (b, i, 0)),
        out_shape=jax.ShapeDtypeStruct((bn, ln, d), F32),
        compiler_params=_cparams(("parallel", "parallel")),
        name="mix_out",
    )(att, z2, z2, z2, x, mod, sgg, sgb, ws, sbias, cvw, cvb, cvg, cvnb, gmat, wout_all)


def _ffn_kernel(x_ref, xp_ref, xn_ref, mod_ref, g_ref, wup_ref, cw_ref, cb_ref, wdn_ref, fg_ref,
                o_ref, act_scr, *, final):
    tm = x_ref.shape[1]
    i = pl.program_id(1)
    nt = pl.num_programs(1)
    shift, scale = mod_ref[0, 3:4, :], mod_ref[0, 4:5, :]
    xm = x_ref[0]
    hm = _norm_mod(xm, g_ref[...], shift, scale).astype(BF16)
    hp = jnp.where(i > 0, _norm_mod(xp_ref[0], g_ref[...], shift, scale), 0.0).astype(BF16)
    hn = jnp.where(i < nt - 1, _norm_mod(xn_ref[0], g_ref[...], shift, scale), 0.0).astype(BF16)
    hh = jnp.concatenate([hp, hm, hn], axis=0)

    n = tm + 2 * FFN_HALO
    for j in range(D_FF // FF_CHUNK):
        cols = slice(j * FF_CHUNK, (j + 1) * FF_CHUNK)
        g = jnp.dot(hh, wup_ref[:, cols], preferred_element_type=F32)
        up = jnp.dot(hm, wup_ref[:, D_FF + j * FF_CHUNK:D_FF + (j + 1) * FF_CHUNK],
                     preferred_element_type=F32)
        g_prev = pltpu.roll(g, 1, axis=0)[FFN_HALO:FFN_HALO + tm]
        g_next = pltpu.roll(g, n - 1, axis=0)[FFN_HALO:FFN_HALO + tm]
        gc = (cb_ref[:, cols] + cw_ref[0:1, cols] * g_prev
              + cw_ref[1:2, cols] * g[FFN_HALO:FFN_HALO + tm] + cw_ref[2:3, cols] * g_next)
        act_scr[:, cols] = (gc * jax.nn.sigmoid(gc) * up).astype(BF16)
    acc = jnp.dot(act_scr[...], wdn_ref[...], preferred_element_type=F32)
    xo = xm + mod_ref[0, 5:6, :] * acc
    if final:
        ms = jnp.mean(xo * xo, axis=-1, keepdims=True)
        xo = xo * lax.rsqrt(ms + EPS) * fg_ref[...]
    o_ref[0] = xo


def _ffn(x, mod, g, wup_all, cw, cb, wdn_all, fg, layer, tm, final):
    bn, ln, d = x.shape
    nh = tm // FFN_HALO
    last = ln // FFN_HALO - 1
    single = dict(pipeline_mode=pl.Buffered(1))
    return pl.pallas_call(
        functools.partial(_ffn_kernel, final=final),
        grid=(bn, ln // tm),
        in_specs=[
            pl.BlockSpec((1, tm, d), lambda b, i: (b, i, 0)),
            pl.BlockSpec((1, FFN_HALO, d), lambda b, i: (b, jnp.maximum(i * nh - 1, 0), 0)),
            pl.BlockSpec((1, FFN_HALO, d), lambda b, i: (b, jnp.minimum((i + 1) * nh, last), 0)),
            pl.BlockSpec((1, 6, d), lambda b, i: (b, 0, 0)),
            _full((1, d)),
            _layer((d, 2 * D_FF), layer, **single),
            _full((FFN_KERNEL, D_FF)), _full((1, D_FF)),
            _layer((D_FF, d), layer, **single),
            _full((1, d)),
        ],
        out_specs=pl.BlockSpec((1, tm, d), lambda b, i: (b, i, 0)),
        out_shape=jax.ShapeDtypeStruct((bn, ln, d), F32),
        scratch_shapes=[pltpu.VMEM((tm, D_FF), BF16)],
        compiler_params=_cparams(("parallel", "parallel")),
        name="conv_ffn",
    )(x, x, x, mod, g, wup_all, cw, cb, wdn_all, fg)


def kernel(x, c, ctx, c_ctx, ada_w, ada_b, norm1_g, w_in, na_rpb, sg_norm_g, sg_norm_b, sg_w,
           sg_b, cv_w, cv_b, cv_norm_g, cv_norm_b, w_out, norm2_g, ffn_w_up, ffn_conv_w,
           ffn_conv_b, ffn_w_down, final_norm_g):
    bn, ln, d = x.shape
    lc = ctx.shape[1]
    assert (d, ln, ln // GRID_W) == (D_MODEL, 2048, 32) and lc % SG_CHUNK == 0

    rows = -(-(bn + 1) // 8) * 8
    c_all = jnp.zeros((rows, d), F32).at[:bn].set(c).at[bn].set(c_ctx)
    mods = _ada_table(c_all, ada_w, ada_b)
    tables = _bias_tables(na_rpb)

    w_in_b = w_in.astype(BF16)
    wvt_b = w_in[:, :, 2 * ATT_W:3 * ATT_W].transpose(0, 2, 1).astype(BF16)
    w_out_b = w_out.astype(BF16)
    w_up_b = ffn_w_up.astype(BF16)
    w_dn_b = ffn_w_down.astype(BF16)
    ws_b = sg_w.reshape(DEPTH, SG_GROUPS * SG_CHUNK, SG_CHUNK).astype(BF16)
    sbias = jnp.repeat(sg_b.transpose(0, 2, 1), SG_W // SG_GROUPS, axis=2)
    grp = np.arange(CV_W) // (CV_W // CV_GROUPS)
    gmat = jnp.asarray((grp[:, None] == grp[None, :]) / (CV_W // CV_GROUPS), BF16)
    fg = final_norm_g.reshape(1, d)

    tm, tmc = 512, lc
    xc = ctx
    for l in range(DEPTH):
        last = l == DEPTH - 1
        mod_l = mods[l, :bn].reshape(bn, 6, d)
        mod_c = jnp.broadcast_to(mods[l, bn].reshape(1, 6, d), (bn, 6, d))
        g1 = norm1_g[l].reshape(1, d)
        g2 = norm2_g[l].reshape(1, d)
        mix_w = (sg_norm_g[l].reshape(1, SG_W), sg_norm_b[l].reshape(1, SG_W), ws_b[l], sbias[l],
                 cv_w[l], cv_b[l].reshape(1, CV_W), cv_norm_g[l].reshape(1, CV_W),
                 cv_norm_b[l].reshape(1, CV_W), gmat, w_out_b, l)
        ffn_w = (w_up_b, ffn_conv_w[l], ffn_conv_b[l].reshape(1, D_FF), w_dn_b, fg, l)

        qkv_c, vt_c, z2_c = _inproj(xc, mod_c, g1, w_in_b, wvt_b, l, tmc, 3)
        qk, vt, z2 = _inproj(x, mod_l, g1, w_in_b, wvt_b, l, tm, 2)
        att = _attention(qk, vt, qkv_c, vt_c, tables, l)
        x = _mix(att, z2, x, mod_l, *mix_w, tm)
        x = _ffn(x, mod_l, g2, *ffn_w, tm, last)
        if not last:
            att_c = _ctx_attention(qkv_c)
            xc = _mix(att_c, z2_c, xc, mod_c, *mix_w, tmc)
            xc = _ffn(xc, mod_c, g2, *ffn_w, tmc, False)
    return x
```

```python
import functools

import jax
import jax.numpy as jnp
import numpy as np
from jax import lax
from jax.experimental import pallas as pl
from jax.experimental.pallas import tpu as pltpu

D_MODEL = 1024
DEPTH = 4
GRID_W = 64
HEAD_DIM = 64
ATT_W = 512
NA_HEADS = 8
NA_ROWS = 8
NA_COLS = 16
SG_W = 256
SG_GROUPS = 4
SG_CHUNK = 128
CV_W = 256
CV_GROUPS = 4
CV_KERNEL = 31
N_IN = 3 * ATT_W + 2 * SG_W + 2 * CV_W
D_FF = 2816
FFN_KERNEL = 3
EPS = 1e-6
NEG_INF = -1e30

LANES = 128
HEAD_PAIRS = ATT_W // LANES
Q_ROWS = 2
K_ROWS = 10
CV_HALO = 16
FFN_HALO = 8
FF_CHUNK = 256
ATT_UNROLL = 8
ATT_AHEAD = 2
LOG2E = 1.4426950408889634
SUM_ROWS = 16
VMEM_LIMIT = 56 * 1024 * 1024

F32 = jnp.float32
BF16 = jnp.bfloat16


def _cparams(sem):
    return pltpu.CompilerParams(dimension_semantics=sem, vmem_limit_bytes=VMEM_LIMIT)


def _full(shape):
    nd = len(shape)
    return pl.BlockSpec(shape, lambda *_: (0,) * nd)


def _layer(shape, layer, **kw):
    nd = len(shape)
    return pl.BlockSpec((None,) + tuple(shape), lambda *_: (layer,) + (0,) * nd, **kw)


def _ada_kernel(c_ref, w_ref, b_ref, o_ref):
    cs = c_ref[...]
    cs = cs * jax.nn.sigmoid(cs)
    o_ref[0] = jnp.dot(cs.astype(BF16), w_ref[0].astype(BF16),
                       preferred_element_type=F32) + b_ref[0]


def _ada_table(c_all, ada_w, ada_b):
    rows = c_all.shape[0]
    tn = 1536
    return pl.pallas_call(
        _ada_kernel,
        grid=(DEPTH, 6 * D_MODEL // tn),
        in_specs=[
            pl.BlockSpec((rows, D_MODEL), lambda l, j: (0, 0)),
            pl.BlockSpec((1, D_MODEL, tn), lambda l, j: (l, 0, j)),
            pl.BlockSpec((1, 1, tn), lambda l, j: (l, 0, j)),
        ],
        out_specs=pl.BlockSpec((1, rows, tn), lambda l, j: (l, 0, j)),
        out_shape=jax.ShapeDtypeStruct((DEPTH, rows, 6 * D_MODEL), F32),
        compiler_params=_cparams(("arbitrary", "arbitrary")),
        name="ada_table",
    )(c_all, ada_w, ada_b.reshape(DEPTH, 1, 6 * D_MODEL))


def _norm_mod(x, g, shift, scale):
    ms = jnp.mean(x * x, axis=-1, keepdims=True)
    return (x * lax.rsqrt(ms + EPS) * g) * (1.0 + scale) + shift


_NT = (((1,), (1,)), ((), ()))


def _inproj_kernel(x_ref, mod_ref, g_ref, w_ref, sgg_ref, sgb_ref,
                   qkv_ref, vt_ref, sg_ref, cvh_ref, *, n_tok):
    tm = x_ref.shape[1]
    h = _norm_mod(x_ref[0], g_ref[...], mod_ref[0, 0:1, :], mod_ref[0, 1:2, :]).astype(BF16)
    z = jnp.dot(h, w_ref[:, 3 * ATT_W:], preferred_element_type=F32)
    v = jnp.dot(h, w_ref[:, 2 * ATT_W:3 * ATT_W], preferred_element_type=F32)
    q = jnp.dot(h, w_ref[:, 0:ATT_W], preferred_element_type=F32)
    k = jnp.dot(h, w_ref[:, ATT_W:2 * ATT_W], preferred_element_type=F32)
    qkv_ref[0, :, 0:ATT_W] = (q * (HEAD_DIM ** -0.5 * LOG2E)).astype(BF16)
    qkv_ref[0, :, ATT_W:2 * ATT_W] = k.astype(BF16)
    if n_tok == 3:
        qkv_ref[0, :, 2 * ATT_W:3 * ATT_W] = v.astype(BF16)
    vt = v.T.astype(BF16)
    for c in range(tm // LANES):
        vt_ref[0, c] = vt[:, c * LANES:(c + 1) * LANES]

    sg_ref[0, :, 0:SG_W] = jax.nn.gelu(z[:, 0:SG_W]).astype(BF16)
    gv = jax.nn.gelu(z[:, SG_W:2 * SG_W])
    mu = jnp.mean(gv, axis=-1, keepdims=True)
    gc = gv - mu
    var = jnp.mean(gc * gc, axis=-1, keepdims=True)
    sg_ref[0, :, SG_W:2 * SG_W] = ((gc * lax.rsqrt(var + EPS)) * sgg_ref[...]
                                   + sgb_ref[...]).astype(BF16)
    a0 = 2 * SG_W
    cvh_ref[0] = z[:, a0:a0 + CV_W] * jax.nn.sigmoid(z[:, a0 + CV_W:a0 + 2 * CV_W])


def _inproj(x, mod, g, w_all, sgg, sgb, layer, tm, n_tok):
    bn, ln, d = x.shape
    nc = tm // LANES
    return pl.pallas_call(
        functools.partial(_inproj_kernel, n_tok=n_tok),
        grid=(bn, ln // tm),
        in_specs=[
            pl.BlockSpec((1, tm, d), lambda b, i: (b, i, 0)),
            pl.BlockSpec((1, 6, d), lambda b, i: (b, 0, 0)),
            _full((1, d)),
            _layer((d, N_IN), layer),
            _full((1, SG_W)), _full((1, SG_W)),
        ],
        out_specs=[
            pl.BlockSpec((1, tm, n_tok * ATT_W), lambda b, i: (b, i, 0)),
            pl.BlockSpec((1, nc, ATT_W, LANES), lambda b, i: (b, i, 0, 0)),
            pl.BlockSpec((1, tm, 2 * SG_W), lambda b, i: (b, i, 0)),
            pl.BlockSpec((1, tm, CV_W), lambda b, i: (b, i, 0)),
        ],
        out_shape=[
            jax.ShapeDtypeStruct((bn, ln, n_tok * ATT_W), BF16),
            jax.ShapeDtypeStruct((bn, ln // LANES, ATT_W, LANES), BF16),
            jax.ShapeDtypeStruct((bn, ln, 2 * SG_W), BF16),
            jax.ShapeDtypeStruct((bn, ln, CV_W), F32),
        ],
        compiler_params=_cparams(("parallel", "parallel")),
        name="inproj",
    )(x, mod, g, w_all, sgg, sgb)


GRID_ROWS = 2048 // GRID_W
RPB_W = 2 * NA_COLS - 1
RPB_PAD = 512
ROW_CFGS = ((0, 0), (2, 0), (8, 4), (GRID_ROWS - 4, GRID_ROWS - K_ROWS),
            (GRID_ROWS - 2, GRID_ROWS - K_ROWS))


def _table_kernel(rpb_ref, o_ref):
    r_base = pl.program_id(0) * NA_HEADS + pl.program_id(1) * 2
    shape = (GRID_W, LANES)
    kc = lax.broadcasted_iota(jnp.int32, shape, 0)
    lane = lax.broadcasted_iota(jnp.int32, shape, 1)
    low = lane < GRID_W
    qc = jnp.where(low, lane, lane - GRID_W)
    off = kc - qc + (NA_COLS - 1)
    sc = jnp.clip(qc - NA_COLS // 2, 0, GRID_W - NA_COLS)
    in_cols = (kc >= sc) & (kc < sc + NA_COLS)
    neg = jnp.full(shape, NEG_INF, F32)
    for hh in range(2):
        toep = []
        for u in range(2 * NA_ROWS - 1):
            acc = jnp.zeros(shape, F32)
            for w in range(RPB_W):
                acc = jnp.where(off == w, rpb_ref[r_base + hh, u * RPB_W + w], acc)
            toep.append(jnp.where(in_cols, acc * LOG2E, neg))
        for cfg, (r0, a) in enumerate(ROW_CFGS):
            for j in range(K_ROWS):
                kr = a + j
                halves = []
                for qr in (r0, r0 + 1):
                    sr = min(max(qr - NA_ROWS // 2, 0), GRID_ROWS - NA_ROWS)
                    inside = sr <= kr < sr + NA_ROWS
                    halves.append(toep[kr - qr + NA_ROWS - 1] if inside else neg)
                o_ref[0, cfg, 0, j * GRID_W:(j + 1) * GRID_W, hh * LANES:(hh + 1) * LANES] = (
                    jnp.where(low, halves[0], halves[1]))


def _bias_tables(na_rpb):
    assert Q_ROWS == 2
    flat = na_rpb.reshape(DEPTH * NA_HEADS, -1)
    flat = jnp.pad(flat, ((0, 0), (0, RPB_PAD - flat.shape[1])))
    nq2, nk = 2 * Q_ROWS * GRID_W, K_ROWS * GRID_W
    return pl.pallas_call(
        _table_kernel,
        grid=(DEPTH, HEAD_PAIRS),
        in_specs=[pl.BlockSpec(memory_space=pltpu.SMEM)],
        out_specs=pl.BlockSpec((1, len(ROW_CFGS), 1, nk, nq2), lambda l, p: (l, 0, p, 0, 0)),
        out_shape=jax.ShapeDtypeStruct((DEPTH, len(ROW_CFGS), HEAD_PAIRS, nk, nq2), F32),
        compiler_params=_cparams(("arbitrary", "arbitrary")),
        name="bias_tables",
    )(flat)


def _stack_heads(q2):
    lane = lax.broadcasted_iota(jnp.int32, q2.shape, 1)
    zero = jnp.zeros_like(q2)
    return jnp.concatenate([jnp.where(lane < HEAD_DIM, q2, zero),
                            jnp.where(lane < HEAD_DIM, zero, q2)], axis=0)


def _unstack_heads(o, n):
    lane = lax.broadcasted_iota(jnp.int32, (n, LANES), 1)
    return jnp.where(lane < HEAD_DIM, o[0:n], o[n:2 * n])


def _attn_kernel(q_ref, k_ref, vt_ref, kc_ref, vct_ref, tab_ref, o_ref):
    nq = Q_ROWS * GRID_W
    nk = K_ROWS * GRID_W
    nrb = q_ref.shape[1] // nq
    rows = q_ref.shape[1] // GRID_W
    row = lax.broadcasted_iota(jnp.int32, (LANES, nq), 0)
    ones_loc = jnp.ones((SUM_ROWS, nk), BF16)
    ones_ctx = jnp.ones((SUM_ROWS, kc_ref.shape[1]), BF16)

    def body(it, carry):
        def place(u):
            rb = it * ATT_UNROLL + u
            c0 = jnp.clip(rb - NA_ROWS // 4, 0, (rows - K_ROWS) // 2)
            cfg = jnp.where(rb < 2, rb, jnp.where(rb < nrb - 2, 2, rb - (nrb - 5)))
            return c0, cfg, pl.multiple_of(rb * nq, nq), pl.multiple_of(c0 * LANES, LANES)

        places = [place(u) for u in range(ATT_UNROLL)]

        def scores(u, p):
            _, cfg, q0, k0 = places[u]
            cols = slice(p * LANES, (p + 1) * LANES)
            qs = _stack_heads(q_ref[0, pl.ds(q0, nq), cols])
            s_loc = lax.dot_general(k_ref[0, pl.ds(k0, nk), cols], qs, _NT,
                                    preferred_element_type=F32) + tab_ref[0, cfg, p]
            s_ctx = lax.dot_general(kc_ref[0, :, cols], qs, _NT, preferred_element_type=F32)
            return s_loc, s_ctx

        steps = [(u, p) for u in range(ATT_UNROLL) for p in range(HEAD_PAIRS)]
        ahead = [scores(*st) for st in steps[:ATT_AHEAD]]
        for i, (u, p) in enumerate(steps):
            c0, _, q0, _ = places[u]
            cols = slice(p * LANES, (p + 1) * LANES)
            s_loc, s_ctx = ahead.pop(0)
            if i + ATT_AHEAD < len(steps):
                ahead.append(scores(*steps[i + ATT_AHEAD]))
            m = jnp.maximum(jnp.max(s_loc, axis=0, keepdims=True),
                            jnp.max(s_ctx, axis=0, keepdims=True))
            p_loc = jnp.exp2(s_loc - m).astype(BF16)
            p_ctx = jnp.exp2(s_ctx - m).astype(BF16)
            vt = jnp.concatenate([vt_ref[0, c0 + c, cols, :] for c in range(nk // LANES)]
                                 , axis=1)
            vct = jnp.concatenate([vct_ref[0, c, cols, :] for c in range(vct_ref.shape[1])],
                                  axis=1)
            o = (jnp.dot(jnp.concatenate([vt, ones_loc], axis=0), p_loc,
                         preferred_element_type=F32)
                 + jnp.dot(jnp.concatenate([vct, ones_ctx], axis=0), p_ctx,
                           preferred_element_type=F32))
            o = o[0:LANES] / o[LANES:LANES + 1]
            o2 = jnp.where(row < HEAD_DIM, o[:, 0:nq], o[:, nq:2 * nq])
            o_ref[0, pl.ds(q0, nq), cols] = o2.T.astype(BF16)
        return carry

    lax.fori_loop(0, nrb // ATT_UNROLL, body, 0)


def _attention(qk, vt, qkv_c, vt_c, tables, layer):
    bn, ln, _ = qk.shape
    lc = qkv_c.shape[1]
    return pl.pallas_call(
        _attn_kernel,
        grid=(bn,),
        in_specs=[
            pl.BlockSpec((1, ln, ATT_W), lambda b: (b, 0, 0)),
            pl.BlockSpec((1, ln, ATT_W), lambda b: (b, 0, 1)),
            pl.BlockSpec((1,) + vt.shape[1:], lambda b: (b, 0, 0, 0)),
            pl.BlockSpec((1, lc, ATT_W), lambda b: (b, 0, 1)),
            pl.BlockSpec((1,) + vt_c.shape[1:], lambda b: (b, 0, 0, 0)),
            pl.BlockSpec((1,) + tables.shape[1:], lambda b: (layer, 0, 0, 0, 0),
                         pipeline_mode=pl.Buffered(1)),
        ],
        out_specs=pl.BlockSpec((1, ln, ATT_W), lambda b: (b, 0, 0)),
        out_shape=jax.ShapeDtypeStruct((bn, ln, ATT_W), BF16),
        compiler_params=_cparams(("parallel",)),
        name="attention",
    )(qk, qk, vt, qkv_c, vt_c, tables)


def _ctx_attn_kernel(q_ref, k_ref, v_ref, o_ref):
    n = q_ref.shape[1]
    for p in range(HEAD_PAIRS):
        cols = slice(p * LANES, (p + 1) * LANES)
        qs = _stack_heads(q_ref[0, :, cols])
        s = lax.dot_general(qs, k_ref[0, :, cols], _NT, preferred_element_type=F32)
        e = jnp.exp2(s - jnp.max(s, axis=-1, keepdims=True))
        den = jnp.sum(e, axis=-1, keepdims=True)
        o = jnp.dot(e.astype(BF16), v_ref[0, :, cols], preferred_element_type=F32)
        o_ref[0, :, cols] = _unstack_heads(o / den, n).astype(BF16)


def _ctx_attention(qkv_c):
    bn, lc, _ = qkv_c.shape
    return pl.pallas_call(
        _ctx_attn_kernel,
        grid=(bn,),
        in_specs=[pl.BlockSpec((1, lc, ATT_W), lambda b, j=j: (b, 0, j)) for j in range(3)],
        out_specs=pl.BlockSpec((1, lc, ATT_W), lambda b: (b, 0, 0)),
        out_shape=jax.ShapeDtypeStruct((bn, lc, ATT_W), BF16),
        compiler_params=_cparams(("parallel",)),
        name="ctx_attention",
    )(qkv_c, qkv_c, qkv_c)


def _mix_kernel(att_ref, sg_ref, h_ref, hp_ref, hn_ref, x_ref, mod_ref, ws_ref,
                sbias_ref, cvw_ref, cvb_ref, cvg_ref, cvnb_ref, gmat_ref, wout_ref,
                o_ref, *, two_seq):
    tm = x_ref.shape[1]
    i = pl.program_id(1)
    nt = pl.num_programs(1)

    lane = lax.broadcasted_iota(jnp.int32, (SG_CHUNK, SG_W), 1)
    gw = SG_W // SG_GROUPS
    mixed = []
    for c in range(tm // SG_CHUNK):
        r = jnp.dot(ws_ref[...], sg_ref[0, c * SG_CHUNK:(c + 1) * SG_CHUNK, SG_W:2 * SG_W],
                    preferred_element_type=F32)
        m = r[0:SG_CHUNK]
        for g in range(1, SG_GROUPS):
            m = jnp.where(lane >= g * gw, r[g * SG_CHUNK:(g + 1) * SG_CHUNK], m)
        mixed.append(m + sbias_ref[...])
    sg = sg_ref[0, :, 0:SG_W].astype(F32) * jnp.concatenate(mixed, axis=0)

    hf = jnp.concatenate([jnp.where(i > 0, hp_ref[0], 0.0), h_ref[0],
                          jnp.where(i < nt - 1, hn_ref[0], 0.0)], axis=0)
    th = tm // 2
    nh = th + 2 * CV_HALO
    zero_halo = jnp.zeros((CV_HALO, CV_W), F32)

    def group_mean(t):
        hi = t.astype(BF16)
        lo = (t - hi.astype(F32)).astype(BF16)
        return (jnp.dot(hi, gmat_ref[...], preferred_element_type=F32)
                + jnp.dot(lo, gmat_ref[...], preferred_element_type=F32))

    for r0 in (0, th):
        rows = slice(r0, r0 + th)
        if two_seq:
            hh = jnp.concatenate([zero_halo, h_ref[0, rows, :], zero_halo], axis=0)
        else:
            hh = hf[r0:r0 + nh]
        shifted = [hh] + [pltpu.roll(hh, nh - b, axis=0) for b in range(1, 8)]
        y = jnp.zeros((th, CV_W), F32) + cvb_ref[...]
        for k in range(CV_KERNEL):
            s = CV_HALO - CV_KERNEL // 2 + k
            y = y + cvw_ref[k:k + 1, :] * shifted[s % 8][8 * (s // 8):8 * (s // 8) + th]
        ymean = group_mean(y)
        out = (jnp.dot(att_ref[0, rows, :], wout_ref[0:ATT_W, :], preferred_element_type=F32)
               + jnp.dot(sg[rows].astype(BF16), wout_ref[ATT_W:ATT_W + SG_W, :],
                         preferred_element_type=F32))
        yc = y - ymean
        yn = yc * lax.rsqrt(group_mean(yc * yc) + EPS) * cvg_ref[...] + cvnb_ref[...]
        cv = yn * jax.nn.sigmoid(yn)
        out = out + jnp.dot(cv.astype(BF16), wout_ref[ATT_W + SG_W:, :],
                            preferred_element_type=F32)
        o_ref[0, rows, :] = x_ref[0, rows, :] + mod_ref[0, 2:3, :] * out


def _mix(att, sg2, cvh, x, mod, ws, sbias, cvw, cvb, cvg, cvnb, gmat, wout_all, layer, tm,
         two_seq=False):
    bn, ln, d = x.shape
    nh = tm // CV_HALO
    last = ln // CV_HALO - 1
    return pl.pallas_call(
        functools.partial(_mix_kernel, two_seq=two_seq),
        grid=(bn, ln // tm),
        in_specs=[
            pl.BlockSpec((1, tm, ATT_W), lambda b, i: (b, i, 0)),
            pl.BlockSpec((1, tm, 2 * SG_W), lambda b, i: (b, i, 0)),
            pl.BlockSpec((1, tm, CV_W), lambda b, i: (b, i, 0)),
            pl.BlockSpec((1, CV_HALO, CV_W), lambda b, i: (b, jnp.maximum(i * nh - 1, 0), 0)),
            pl.BlockSpec((1, CV_HALO, CV_W), lambda b, i: (b, jnp.minimum((i + 1) * nh, last), 0)),
            pl.BlockSpec((1, tm, d), lambda b, i: (b, i, 0)),
            pl.BlockSpec((1, 6, d), lambda b, i: (b, 0, 0)),
            _full((SG_GROUPS * SG_CHUNK, SG_CHUNK)),
            _full((SG_CHUNK, SG_W)),
            _full((CV_KERNEL, CV_W)), _full((1, CV_W)), _full((1, CV_W)), _full((1, CV_W)),
            _full((CV_W, CV_W)),
            _layer((d, d), layer),
        ],
        out_specs=pl.BlockSpec((1, tm, d), lambda b, i: (b, i, 0)),
        out_shape=jax.ShapeDtypeStruct((bn, ln, d), F32),
        compiler_params=_cparams(("parallel", "parallel")),
        name="mix_out",
    )(att, sg2, cvh, cvh, cvh, x, mod, ws, sbias, cvw, cvb, cvg, cvnb, gmat, wout_all)


def _ffn_kernel(x_ref, xp_ref, xn_ref, mod_ref, g_ref, wup_ref, cw_ref, cb_ref, wdn_ref, fg_ref,
                o_ref, act_scr, *, final, two_seq):
    tm = x_ref.shape[1]
    i = pl.program_id(1)
    nt = pl.num_programs(1)
    shift, scale = mod_ref[0, 3:4, :], mod_ref[0, 4:5, :]
    xm = x_ref[0]
    hm = _norm_mod(xm, g_ref[...], shift, scale).astype(BF16)
    hp = jnp.where(i > 0, _norm_mod(xp_ref[0], g_ref[...], shift, scale), 0.0).astype(BF16)
    hn = jnp.where(i < nt - 1, _norm_mod(xn_ref[0], g_ref[...], shift, scale), 0.0).astype(BF16)
    hh = jnp.concatenate([hp, hm, hn], axis=0)

    n = tm + 2 * FFN_HALO
    row = lax.broadcasted_iota(jnp.int32, (tm, 1), 0)
    for j in range(D_FF // FF_CHUNK):
        cols = slice(j * FF_CHUNK, (j + 1) * FF_CHUNK)
        g = jnp.dot(hh, wup_ref[:, cols], preferred_element_type=F32)
        up = jnp.dot(hm, wup_ref[:, D_FF + j * FF_CHUNK:D_FF + (j + 1) * FF_CHUNK],
                     preferred_element_type=F32)
        g_prev = pltpu.roll(g, 1, axis=0)[FFN_HALO:FFN_HALO + tm]
        g_next = pltpu.roll(g, n - 1, axis=0)[FFN_HALO:FFN_HALO + tm]
        if two_seq:
            g_prev = jnp.where(row == tm // 2, 0.0, g_prev)
            g_next = jnp.where(row == tm // 2 - 1, 0.0, g_next)
        gc = (cb_ref[:, cols] + cw_ref[0:1, cols] * g_prev
              + cw_ref[1:2, cols] * g[FFN_HALO:FFN_HALO + tm] + cw_ref[2:3, cols] * g_next)
        act_scr[:, cols] = (gc * jax.nn.sigmoid(gc) * up).astype(BF16)
    acc = jnp.dot(act_scr[...], wdn_ref[...], preferred_element_type=F32)
    xo = xm + mod_ref[0, 5:6, :] * acc
    if final:
        ms = jnp.mean(xo * xo, axis=-1, keepdims=True)
        xo = xo * lax.rsqrt(ms + EPS) * fg_ref[...]
    o_ref[0] = xo


def _ffn(x, mod, g, wup_all, cw, cb, wdn_all, fg, layer, tm, final, two_seq=False):
    bn, ln, d = x.shape
    nh = tm // FFN_HALO
    last = ln // FFN_HALO - 1
    single = dict(pipeline_mode=pl.Buffered(1))
    return pl.pallas_call(
        functools.partial(_ffn_kernel, final=final, two_seq=two_seq),
        grid=(bn, ln // tm),
        in_specs=[
            pl.BlockSpec((1, tm, d), lambda b, i: (b, i, 0)),
            pl.BlockSpec((1, FFN_HALO, d), lambda b, i: (b, jnp.maximum(i * nh - 1, 0), 0)),
            pl.BlockSpec((1, FFN_HALO, d), lambda b, i: (b, jnp.minimum((i + 1) * nh, last), 0)),
            pl.BlockSpec((1, 6, d), lambda b, i: (b, 0, 0)),
            _full((1, d)),
            _layer((d, 2 * D_FF), layer, **single),
            _full((FFN_KERNEL, D_FF)), _full((1, D_FF)),
            _layer((D_FF, d), layer, **single),
            _full((1, d)),
        ],
        out_specs=pl.BlockSpec((1, tm, d), lambda b, i: (b, i, 0)),
        out_shape=jax.ShapeDtypeStruct((bn, ln, d), F32),
        scratch_shapes=[pltpu.VMEM((tm, D_FF), BF16)],
        compiler_params=_cparams(("parallel", "parallel")),
        name="conv_ffn",
    )(x, x, x, mod, g, wup_all, cw, cb, wdn_all, fg)


def kernel(x, c, ctx, c_ctx, ada_w, ada_b, norm1_g, w_in, na_rpb, sg_norm_g, sg_norm_b, sg_w,
           sg_b, cv_w, cv_b, cv_norm_g, cv_norm_b, w_out, norm2_g, ffn_w_up, ffn_conv_w,
           ffn_conv_b, ffn_w_down, final_norm_g):
    bn, ln, d = x.shape
    lc = ctx.shape[1]
    assert (d, ln, ln // GRID_W) == (D_MODEL, 2048, 32) and lc % SG_CHUNK == 0

    rows = -(-(bn + 1) // 8) * 8
    c_all = jnp.zeros((rows, d), F32).at[:bn].set(c).at[bn].set(c_ctx)
    mods = _ada_table(c_all, ada_w, ada_b)
    tables = _bias_tables(na_rpb)

    w_in_b = w_in.astype(BF16)
    w_out_b = w_out.astype(BF16)
    w_up_b = ffn_w_up.astype(BF16)
    w_dn_b = ffn_w_down.astype(BF16)
    ws_b = sg_w.reshape(DEPTH, SG_GROUPS * SG_CHUNK, SG_CHUNK).astype(BF16)
    sbias = jnp.repeat(sg_b.transpose(0, 2, 1), SG_W // SG_GROUPS, axis=2)
    grp = np.arange(CV_W) // (CV_W // CV_GROUPS)
    gmat = jnp.asarray((grp[:, None] == grp[None, :]) / (CV_W // CV_GROUPS), BF16)
    fg = final_norm_g.reshape(1, d)

    tm = 512
    assert bn % 2 == 0 and 2 * lc == tm
    bp = bn // 2

    def pair(t):
        return t.reshape((bp, 2 * lc) + t.shape[2:])

    def unpair(t):
        return t.reshape((bn, lc) + t.shape[2:])

    xc = pair(ctx)
    for l in range(DEPTH):
        last = l == DEPTH - 1
        mod_l = mods[l, :bn].reshape(bn, 6, d)
        mod_c = jnp.broadcast_to(mods[l, bn].reshape(1, 6, d), (bp, 6, d))
        g1 = norm1_g[l].reshape(1, d)
        g2 = norm2_g[l].reshape(1, d)
        in_w = (w_in_b, sg_norm_g[l].reshape(1, SG_W), sg_norm_b[l].reshape(1, SG_W), l)
        mix_w = (ws_b[l], sbias[l], cv_w[l], cv_b[l].reshape(1, CV_W),
                 cv_norm_g[l].reshape(1, CV_W), cv_norm_b[l].reshape(1, CV_W), gmat, w_out_b, l)
        ffn_w = (w_up_b, ffn_conv_w[l], ffn_conv_b[l].reshape(1, D_FF), w_dn_b, fg, l)

        qkv_c, vt_c, sg_c, cvh_c = _inproj(xc, mod_c, g1, *in_w, tm, 3)
        qkv_c = unpair(qkv_c)
        vt_c = vt_c.reshape(bn, lc // LANES, ATT_W, LANES)
        qk, vt, sg2, cvh = _inproj(x, mod_l, g1, *in_w, tm, 2)
        att = _attention(qk, vt, qkv_c, vt_c, tables, l)
        x = _mix(att, sg2, cvh, x, mod_l, *mix_w, tm)
        x = _ffn(x, mod_l, g2, *ffn_w, tm, last)
        if not last:
            att_c = pair(_ctx_attention(qkv_c))
            xc = _mix(att_c, sg_c, cvh_c, xc, mod_c, *mix_w, tm, two_seq=True)
            xc = _ffn(xc, mod_c, g2, *ffn_w, tm, False, two_seq=True)
    return x
```

```python
import functools

import jax
import jax.numpy as jnp
import numpy as np
from jax import lax
from jax.experimental import pallas as pl
from jax.experimental.pallas import tpu as pltpu

D_MODEL = 1024
DEPTH = 4
GRID_W = 64
HEAD_DIM = 64
ATT_W = 512
NA_HEADS = 8
NA_ROWS = 8
NA_COLS = 16
SG_W = 256
SG_GROUPS = 4
SG_CHUNK = 128
CV_W = 256
CV_GROUPS = 4
CV_KERNEL = 31
N_IN = 3 * ATT_W + 2 * SG_W + 2 * CV_W
D_FF = 2816
FFN_KERNEL = 3
EPS = 1e-6
NEG_INF = -1e30

LANES = 128
HEAD_PAIRS = ATT_W // LANES
Q_ROWS = 2
K_ROWS = 10
CV_HALO = 16
FFN_HALO = 8
FF_CHUNK = 256
ATT_UNROLL = 16
ATT_AHEAD = 2
LOG2E = 1.4426950408889634
SUM_ROWS = 16
VMEM_LIMIT = 56 * 1024 * 1024

F32 = jnp.float32
BF16 = jnp.bfloat16


def _cparams(sem):
    return pltpu.CompilerParams(dimension_semantics=sem, vmem_limit_bytes=VMEM_LIMIT)


def _full(shape):
    nd = len(shape)
    return pl.BlockSpec(shape, lambda *_: (0,) * nd)


def _layer(shape, layer, **kw):
    nd = len(shape)
    return pl.BlockSpec((None,) + tuple(shape), lambda *_: (layer,) + (0,) * nd, **kw)


def _ada_kernel(c_ref, w_ref, b_ref, o_ref):
    cs = c_ref[...]
    cs = cs * jax.nn.sigmoid(cs)
    o_ref[0] = jnp.dot(cs.astype(BF16), w_ref[0].astype(BF16),
                       preferred_element_type=F32) + b_ref[0]


def _ada_table(c_all, ada_w, ada_b):
    rows = c_all.shape[0]
    tn = 1536
    return pl.pallas_call(
        _ada_kernel,
        grid=(DEPTH, 6 * D_MODEL // tn),
        in_specs=[
            pl.BlockSpec((rows, D_MODEL), lambda l, j: (0, 0)),
            pl.BlockSpec((1, D_MODEL, tn), lambda l, j: (l, 0, j)),
            pl.BlockSpec((1, 1, tn), lambda l, j: (l, 0, j)),
        ],
        out_specs=pl.BlockSpec((1, rows, tn), lambda l, j: (l, 0, j)),
        out_shape=jax.ShapeDtypeStruct((DEPTH, rows, 6 * D_MODEL), F32),
        compiler_params=_cparams(("arbitrary", "arbitrary")),
        name="ada_table",
    )(c_all, ada_w, ada_b.reshape(DEPTH, 1, 6 * D_MODEL))


def _norm_mod(x, g, shift, scale):
    ms = jnp.mean(x * x, axis=-1, keepdims=True)
    return (x * lax.rsqrt(ms + EPS) * g) * (1.0 + scale) + shift


_NT = (((1,), (1,)), ((), ()))


def _inproj_kernel(x_ref, mod_ref, g_ref, w_ref, sgg_ref, sgb_ref,
                   qkv_ref, vt_ref, sg_ref, cvh_ref, *, n_tok):
    tm = x_ref.shape[1]
    h = _norm_mod(x_ref[0], g_ref[...], mod_ref[0, 0:1, :], mod_ref[0, 1:2, :]).astype(BF16)
    z = jnp.dot(h, w_ref[:, 3 * ATT_W:], preferred_element_type=F32)
    v = jnp.dot(h, w_ref[:, 2 * ATT_W:3 * ATT_W], preferred_element_type=F32)
    q = jnp.dot(h, w_ref[:, 0:ATT_W], preferred_element_type=F32)
    k = jnp.dot(h, w_ref[:, ATT_W:2 * ATT_W], preferred_element_type=F32)
    qkv_ref[0, :, 0:ATT_W] = (q * (HEAD_DIM ** -0.5 * LOG2E)).astype(BF16)
    qkv_ref[0, :, ATT_W:2 * ATT_W] = k.astype(BF16)
    if n_tok == 3:
        qkv_ref[0, :, 2 * ATT_W:3 * ATT_W] = v.astype(BF16)
    vt = v.T.astype(BF16)
    for c in range(tm // LANES):
        vt_ref[0, c] = vt[:, c * LANES:(c + 1) * LANES]

    sg_ref[0, :, 0:SG_W] = jax.nn.gelu(z[:, 0:SG_W]).astype(BF16)
    gv = jax.nn.gelu(z[:, SG_W:2 * SG_W])
    mu = jnp.mean(gv, axis=-1, keepdims=True)
    gc = gv - mu
    var = jnp.mean(gc * gc, axis=-1, keepdims=True)
    sg_ref[0, :, SG_W:2 * SG_W] = ((gc * lax.rsqrt(var + EPS)) * sgg_ref[...]
                                   + sgb_ref[...]).astype(BF16)
    a0 = 2 * SG_W
    cvh_ref[0] = z[:, a0:a0 + CV_W] * jax.nn.sigmoid(z[:, a0 + CV_W:a0 + 2 * CV_W])


def _inproj(x, mod, g, w_all, sgg, sgb, layer, tm, n_tok):
    bn, ln, d = x.shape
    nc = tm // LANES
    return pl.pallas_call(
        functools.partial(_inproj_kernel, n_tok=n_tok),
        grid=(bn, ln // tm),
        in_specs=[
            pl.BlockSpec((1, tm, d), lambda b, i: (b, i, 0)),
            pl.BlockSpec((1, 6, d), lambda b, i: (b, 0, 0)),
            _full((1, d)),
            _layer((d, N_IN), layer),
            _full((1, SG_W)), _full((1, SG_W)),
        ],
        out_specs=[
            pl.BlockSpec((1, tm, n_tok * ATT_W), lambda b, i: (b, i, 0)),
            pl.BlockSpec((1, nc, ATT_W, LANES), lambda b, i: (b, i, 0, 0)),
            pl.BlockSpec((1, tm, 2 * SG_W), lambda b, i: (b, i, 0)),
            pl.BlockSpec((1, tm, CV_W), lambda b, i: (b, i, 0)),
        ],
        out_shape=[
            jax.ShapeDtypeStruct((bn, ln, n_tok * ATT_W), BF16),
            jax.ShapeDtypeStruct((bn, ln // LANES, ATT_W, LANES), BF16),
            jax.ShapeDtypeStruct((bn, ln, 2 * SG_W), BF16),
            jax.ShapeDtypeStruct((bn, ln, CV_W), F32),
        ],
        compiler_params=_cparams(("parallel", "parallel")),
        name="inproj",
    )(x, mod, g, w_all, sgg, sgb)


GRID_ROWS = 2048 // GRID_W
RPB_W = 2 * NA_COLS - 1
RPB_PAD = 512
ROW_CFGS = ((0, 0), (2, 0), (8, 4), (GRID_ROWS - 4, GRID_ROWS - K_ROWS),
            (GRID_ROWS - 2, GRID_ROWS - K_ROWS))


def _table_kernel(rpb_ref, o_ref):
    r_base = pl.program_id(0) * NA_HEADS + pl.program_id(1) * 2
    shape = (GRID_W, LANES)
    kc = lax.broadcasted_iota(jnp.int32, shape, 0)
    lane = lax.broadcasted_iota(jnp.int32, shape, 1)
    low = lane < GRID_W
    qc = jnp.where(low, lane, lane - GRID_W)
    off = kc - qc + (NA_COLS - 1)
    sc = jnp.clip(qc - NA_COLS // 2, 0, GRID_W - NA_COLS)
    in_cols = (kc >= sc) & (kc < sc + NA_COLS)
    neg = jnp.full(shape, NEG_INF, F32)
    for hh in range(2):
        toep = []
        for u in range(2 * NA_ROWS - 1):
            acc = jnp.zeros(shape, F32)
            for w in range(RPB_W):
                acc = jnp.where(off == w, rpb_ref[r_base + hh, u * RPB_W + w], acc)
            toep.append(jnp.where(in_cols, acc * LOG2E, neg))
        for cfg, (r0, a) in enumerate(ROW_CFGS):
            for j in range(K_ROWS):
                kr = a + j
                halves = []
                for qr in (r0, r0 + 1):
                    sr = min(max(qr - NA_ROWS // 2, 0), GRID_ROWS - NA_ROWS)
                    inside = sr <= kr < sr + NA_ROWS
                    halves.append(toep[kr - qr + NA_ROWS - 1] if inside else neg)
                o_ref[0, cfg, 0, j * GRID_W:(j + 1) * GRID_W, hh * LANES:(hh + 1) * LANES] = (
                    jnp.where(low, halves[0], halves[1]))


def _bias_tables(na_rpb):
    assert Q_ROWS == 2
    flat = na_rpb.reshape(DEPTH * NA_HEADS, -1)
    flat = jnp.pad(flat, ((0, 0), (0, RPB_PAD - flat.shape[1])))
    nq2, nk = 2 * Q_ROWS * GRID_W, K_ROWS * GRID_W
    return pl.pallas_call(
        _table_kernel,
        grid=(DEPTH, HEAD_PAIRS),
        in_specs=[pl.BlockSpec(memory_space=pltpu.SMEM)],
        out_specs=pl.BlockSpec((1, len(ROW_CFGS), 1, nk, nq2), lambda l, p: (l, 0, p, 0, 0)),
        out_shape=jax.ShapeDtypeStruct((DEPTH, len(ROW_CFGS), HEAD_PAIRS, nk, nq2), F32),
        compiler_params=_cparams(("arbitrary", "arbitrary")),
        name="bias_tables",
    )(flat)


def _stack_heads(q2):
    lane = lax.broadcasted_iota(jnp.int32, q2.shape, 1)
    zero = jnp.zeros_like(q2)
    return jnp.concatenate([jnp.where(lane < HEAD_DIM, q2, zero),
                            jnp.where(lane < HEAD_DIM, zero, q2)], axis=0)


def _unstack_heads(o, n):
    lane = lax.broadcasted_iota(jnp.int32, (n, LANES), 1)
    return jnp.where(lane < HEAD_DIM, o[0:n], o[n:2 * n])


def _attn_kernel(q_ref, k_ref, vt_ref, kc_ref, vct_ref, tab_ref, o_ref):
    nq = Q_ROWS * GRID_W
    nk = K_ROWS * GRID_W
    nrb = q_ref.shape[1] // nq
    rows = q_ref.shape[1] // GRID_W
    row = lax.broadcasted_iota(jnp.int32, (LANES, nq), 0)
    ones_loc = jnp.ones((SUM_ROWS, nk), BF16)
    ones_ctx = jnp.ones((SUM_ROWS, kc_ref.shape[1]), BF16)

    def body(it, carry):
        def place(u):
            rb = it * ATT_UNROLL + u
            c0 = jnp.clip(rb - NA_ROWS // 4, 0, (rows - K_ROWS) // 2)
            cfg = jnp.where(rb < 2, rb, jnp.where(rb < nrb - 2, 2, rb - (nrb - 5)))
            return c0, cfg, pl.multiple_of(rb * nq, nq), pl.multiple_of(c0 * LANES, LANES)

        places = [place(u) for u in range(ATT_UNROLL)]

        def scores(u, p):
            _, cfg, q0, k0 = places[u]
            cols = slice(p * LANES, (p + 1) * LANES)
            qs = _stack_heads(q_ref[0, pl.ds(q0, nq), cols])
            s_loc = lax.dot_general(k_ref[0, pl.ds(k0, nk), cols], qs, _NT,
                                    preferred_element_type=F32) + tab_ref[0, cfg, p]
            s_ctx = lax.dot_general(kc_ref[0, :, cols], qs, _NT, preferred_element_type=F32)
            return s_loc, s_ctx

        steps = [(u, p) for u in range(ATT_UNROLL) for p in range(HEAD_PAIRS)]
        ahead = [scores(*st) for st in steps[:ATT_AHEAD]]
        for i, (u, p) in enumerate(steps):
            c0, _, q0, _ = places[u]
            cols = slice(p * LANES, (p + 1) * LANES)
            s_loc, s_ctx = ahead.pop(0)
            if i + ATT_AHEAD < len(steps):
                ahead.append(scores(*steps[i + ATT_AHEAD]))
            m = jnp.maximum(jnp.max(s_loc, axis=0, keepdims=True),
                            jnp.max(s_ctx, axis=0, keepdims=True))
            p_loc = jnp.exp2(s_loc - m).astype(BF16)
            p_ctx = jnp.exp2(s_ctx - m).astype(BF16)
            vt = jnp.concatenate([vt_ref[0, c0 + c, cols, :] for c in range(nk // LANES)]
                                 , axis=1)
            vct = jnp.concatenate([vct_ref[0, c, cols, :] for c in range(vct_ref.shape[1])],
                                  axis=1)
            o = (jnp.dot(jnp.concatenate([vt, ones_loc], axis=0), p_loc,
                         preferred_element_type=F32)
                 + jnp.dot(jnp.concatenate([vct, ones_ctx], axis=0), p_ctx,
                           preferred_element_type=F32))
            o = o[0:LANES] / o[LANES:LANES + 1]
            o2 = jnp.where(row < HEAD_DIM, o[:, 0:nq], o[:, nq:2 * nq])
            o_ref[0, pl.ds(q0, nq), cols] = o2.T.astype(BF16)
        return carry

    lax.fori_loop(0, nrb // ATT_UNROLL, body, 0)


def _attention(qk, vt, qkv_c, vt_c, tables, layer):
    bn, ln, _ = qk.shape
    lc = qkv_c.shape[1]
    return pl.pallas_call(
        _attn_kernel,
        grid=(bn,),
        in_specs=[
            pl.BlockSpec((1, ln, ATT_W), lambda b: (b, 0, 0)),
            pl.BlockSpec((1, ln, ATT_W), lambda b: (b, 0, 1)),
            pl.BlockSpec((1,) + vt.shape[1:], lambda b: (b, 0, 0, 0)),
            pl.BlockSpec((1, lc, ATT_W), lambda b: (b, 0, 1)),
            pl.BlockSpec((1,) + vt_c.shape[1:], lambda b: (b, 0, 0, 0)),
            pl.BlockSpec((1,) + tables.shape[1:], lambda b: (layer, 0, 0, 0, 0),
                         pipeline_mode=pl.Buffered(1)),
        ],
        out_specs=pl.BlockSpec((1, ln, ATT_W), lambda b: (b, 0, 0)),
        out_shape=jax.ShapeDtypeStruct((bn, ln, ATT_W), BF16),
        compiler_params=_cparams(("parallel",)),
        name="attention",
    )(qk, qk, vt, qkv_c, vt_c, tables)


def _ctx_attn_kernel(q_ref, k_ref, v_ref, o_ref):
    n = q_ref.shape[1]
    for p in range(HEAD_PAIRS):
        cols = slice(p * LANES, (p + 1) * LANES)
        qs = _stack_heads(q_ref[0, :, cols])
        s = lax.dot_general(qs, k_ref[0, :, cols], _NT, preferred_element_type=F32)
        e = jnp.exp2(s - jnp.max(s, axis=-1, keepdims=True))
        den = jnp.sum(e, axis=-1, keepdims=True)
        o = jnp.dot(e.astype(BF16), v_ref[0, :, cols], preferred_element_type=F32)
        o_ref[0, :, cols] = _unstack_heads(o / den, n).astype(BF16)


def _ctx_attention(qkv_c):
    bn, lc, _ = qkv_c.shape
    return pl.pallas_call(
        _ctx_attn_kernel,
        grid=(bn,),
        in_specs=[pl.BlockSpec((1, lc, ATT_W), lambda b, j=j: (b, 0, j)) for j in range(3)],
        out_specs=pl.BlockSpec((1, lc, ATT_W), lambda b: (b, 0, 0)),
        out_shape=jax.ShapeDtypeStruct((bn, lc, ATT_W), BF16),
        compiler_params=_cparams(("parallel",)),
        name="ctx_attention",
    )(qkv_c, qkv_c, qkv_c)


def _mix_kernel(att_ref, sg_ref, h_ref, hp_ref, hn_ref, x_ref, mod_ref, ws_ref,
                sbias_ref, cvw_ref, cvb_ref, cvg_ref, cvnb_ref, gmat_ref, wout_ref,
                o_ref, *, two_seq):
    tm = x_ref.shape[1]
    i = pl.program_id(1)
    nt = pl.num_programs(1)

    grp = lax.broadcasted_iota(jnp.int32, (SG_CHUNK, SG_W), 1) // (SG_W // SG_GROUPS)
    mixed = []
    for c in range(tm // SG_CHUNK):
        vch = sg_ref[0, c * SG_CHUNK:(c + 1) * SG_CHUNK, SG_W:2 * SG_W]
        vbd = jnp.concatenate([jnp.where(grp == g, vch, jnp.zeros_like(vch))
                               for g in range(SG_GROUPS)], axis=0)
        mixed.append(jnp.dot(ws_ref[...], vbd, preferred_element_type=F32) + sbias_ref[...])
    sg = sg_ref[0, :, 0:SG_W].astype(F32) * jnp.concatenate(mixed, axis=0)

    hf = jnp.concatenate([jnp.where(i > 0, hp_ref[0], 0.0), h_ref[0],
                          jnp.where(i < nt - 1, hn_ref[0], 0.0)], axis=0)
    th = tm // 2
    nh = th + 2 * CV_HALO
    zero_halo = jnp.zeros((CV_HALO, CV_W), F32)

    def group_mean(t):
        hi = t.astype(BF16)
        lo = (t - hi.astype(F32)).astype(BF16)
        return (jnp.dot(hi, gmat_ref[...], preferred_element_type=F32)
                + jnp.dot(lo, gmat_ref[...], preferred_element_type=F32))

    halves = [slice(0, th), slice(th, tm)]
    ys, ymean, out = [], [], []
    for rows in halves:
        r0 = rows.start
        if two_seq:
            hh = jnp.concatenate([zero_halo, h_ref[0, rows, :], zero_halo], axis=0)
        else:
            hh = hf[r0:r0 + nh]
        shifted = [hh] + [pltpu.roll(hh, nh - b, axis=0) for b in range(1, 8)]
        y = jnp.zeros((th, CV_W), F32) + cvb_ref[...]
        for k in range(CV_KERNEL):
            s = CV_HALO - CV_KERNEL // 2 + k
            y = y + cvw_ref[k:k + 1, :] * shifted[s % 8][8 * (s // 8):8 * (s // 8) + th]
        ys.append(y)
        ymean.append(group_mean(y))
        out.append(jnp.dot(att_ref[0, rows, :], wout_ref[0:ATT_W, :], preferred_element_type=F32)
                   + jnp.dot(sg[rows].astype(BF16), wout_ref[ATT_W:ATT_W + SG_W, :],
                             preferred_element_type=F32))
    yc = [y - m for y, m in zip(ys, ymean)]
    yvar = [group_mean(t * t) for t in yc]
    for rows, o, t, var in zip(halves, out, yc, yvar):
        yn = t * lax.rsqrt(var + EPS) * cvg_ref[...] + cvnb_ref[...]
        cv = yn * jax.nn.sigmoid(yn)
        o = o + jnp.dot(cv.astype(BF16), wout_ref[ATT_W + SG_W:, :], preferred_element_type=F32)
        o_ref[0, rows, :] = x_ref[0, rows, :] + mod_ref[0, 2:3, :] * o


def _mix(att, sg2, cvh, x, mod, ws, sbias, cvw, cvb, cvg, cvnb, gmat, wout_all, layer, tm,
         two_seq=False):
    bn, ln, d = x.shape
    nh = tm // CV_HALO
    last = ln // CV_HALO - 1
    return pl.pallas_call(
        functools.partial(_mix_kernel, two_seq=two_seq),
        grid=(bn, ln // tm),
        in_specs=[
            pl.BlockSpec((1, tm, ATT_W), lambda b, i: (b, i, 0)),
            pl.BlockSpec((1, tm, 2 * SG_W), lambda b, i: (b, i, 0)),
            pl.BlockSpec((1, tm, CV_W), lambda b, i: (b, i, 0)),
            pl.BlockSpec((1, CV_HALO, CV_W), lambda b, i: (b, jnp.maximum(i * nh - 1, 0), 0)),
            pl.BlockSpec((1, CV_HALO, CV_W), lambda b, i: (b, jnp.minimum((i + 1) * nh, last), 0)),
            pl.BlockSpec((1, tm, d), lambda b, i: (b, i, 0)),
            pl.BlockSpec((1, 6, d), lambda b, i: (b, 0, 0)),
            _full((SG_CHUNK, SG_GROUPS * SG_CHUNK)),
            _full((SG_CHUNK, SG_W)),
            _full((CV_KERNEL, CV_W)), _full((1, CV_W)), _full((1, CV_W)), _full((1, CV_W)),
            _full((CV_W, CV_W)),
            _layer((d, d), layer),
        ],
        out_specs=pl.BlockSpec((1, tm, d), lambda b, i: (b, i, 0)),
        out_shape=jax.ShapeDtypeStruct((bn, ln, d), F32),
        compiler_params=_cparams(("parallel", "parallel")),
        name="mix_out",
    )(att, sg2, cvh, cvh, cvh, x, mod, ws, sbias, cvw, cvb, cvg, cvnb, gmat, wout_all)


def _ffn_kernel(x_ref, xp_ref, xn_ref, mod_ref, g_ref, wup_ref, cw_ref, cb_ref, wdn_ref, fg_ref,
                o_ref, act_scr, *, final, two_seq):
    tm = x_ref.shape[1]
    i = pl.program_id(1)
    nt = pl.num_programs(1)
    shift, scale = mod_ref[0, 3:4, :], mod_ref[0, 4:5, :]
    xm = x_ref[0]
    hm = _norm_mod(xm, g_ref[...], shift, scale).astype(BF16)
    hp = jnp.where(i > 0, _norm_mod(xp_ref[0], g_ref[...], shift, scale), 0.0).astype(BF16)
    hn = jnp.where(i < nt - 1, _norm_mod(xn_ref[0], g_ref[...], shift, scale), 0.0).astype(BF16)
    hh = jnp.concatenate([hp, hm, hn], axis=0)

    n = tm + 2 * FFN_HALO
    row = lax.broadcasted_iota(jnp.int32, (tm, 1), 0)
    for j in range(D_FF // FF_CHUNK):
        cols = slice(j * FF_CHUNK, (j + 1) * FF_CHUNK)
        g = jnp.dot(hh, wup_ref[:, cols], preferred_element_type=F32)
        up = jnp.dot(hm, wup_ref[:, D_FF + j * FF_CHUNK:D_FF + (j + 1) * FF_CHUNK],
                     preferred_element_type=F32)
        g_prev = pltpu.roll(g, 1, axis=0)[FFN_HALO:FFN_HALO + tm]
        g_next = pltpu.roll(g, n - 1, axis=0)[FFN_HALO:FFN_HALO + tm]
        if two_seq:
            g_prev = jnp.where(row == tm // 2, 0.0, g_prev)
            g_next = jnp.where(row == tm // 2 - 1, 0.0, g_next)
        gc = (cb_ref[:, cols] + cw_ref[0:1, cols] * g_prev
              + cw_ref[1:2, cols] * g[FFN_HALO:FFN_HALO + tm] + cw_ref[2:3, cols] * g_next)
        act_scr[:, cols] = (gc * jax.nn.sigmoid(gc) * up).astype(BF16)
    acc = jnp.dot(act_scr[...], wdn_ref[...], preferred_element_type=F32)
    xo = xm + mod_ref[0, 5:6, :] * acc
    if final:
        ms = jnp.mean(xo * xo, axis=-1, keepdims=True)
        xo = xo * lax.rsqrt(ms + EPS) * fg_ref[...]
    o_ref[0] = xo


def _ffn(x, mod, g, wup_all, cw, cb, wdn_all, fg, layer, tm, final, two_seq=False):
    bn, ln, d = x.shape
    nh = tm // FFN_HALO
    last = ln // FFN_HALO - 1
    single = dict(pipeline_mode=pl.Buffered(1))
    return pl.pallas_call(
        functools.partial(_ffn_kernel, final=final, two_seq=two_seq),
        grid=(bn, ln // tm),
        in_specs=[
            pl.BlockSpec((1, tm, d), lambda b, i: (b, i, 0)),
            pl.BlockSpec((1, FFN_HALO, d), lambda b, i: (b, jnp.maximum(i * nh - 1, 0), 0)),
            pl.BlockSpec((1, FFN_HALO, d), lambda b, i: (b, jnp.minimum((i + 1) * nh, last), 0)),
            pl.BlockSpec((1, 6, d), lambda b, i: (b, 0, 0)),
            _full((1, d)),
            _layer((d, 2 * D_FF), layer, **single),
            _full((FFN_KERNEL, D_FF)), _full((1, D_FF)),
            _layer((D_FF, d), layer, **single),
            _full((1, d)),
        ],
        out_specs=pl.BlockSpec((1, tm, d), lambda b, i: (b, i, 0)),
        out_shape=jax.ShapeDtypeStruct((bn, ln, d), F32),
        scratch_shapes=[pltpu.VMEM((tm, D_FF), BF16)],
        compiler_params=_cparams(("parallel", "parallel")),
        name="conv_ffn",
    )(x, x, x, mod, g, wup_all, cw, cb, wdn_all, fg)


def kernel(x, c, ctx, c_ctx, ada_w, ada_b, norm1_g, w_in, na_rpb, sg_norm_g, sg_norm_b, sg_w,
           sg_b, cv_w, cv_b, cv_norm_g, cv_norm_b, w_out, norm2_g, ffn_w_up, ffn_conv_w,
           ffn_conv_b, ffn_w_down, final_norm_g):
    bn, ln, d = x.shape
    lc = ctx.shape[1]
    assert (d, ln, ln // GRID_W) == (D_MODEL, 2048, 32) and lc % SG_CHUNK == 0

    rows = -(-(bn + 1) // 8) * 8
    c_all = jnp.zeros((rows, d), F32).at[:bn].set(c).at[bn].set(c_ctx)
    mods = _ada_table(c_all, ada_w, ada_b)
    tables = _bias_tables(na_rpb)

    w_in_b = w_in.astype(BF16)
    w_out_b = w_out.astype(BF16)
    w_up_b = ffn_w_up.astype(BF16)
    w_dn_b = ffn_w_down.astype(BF16)
    ws_b = sg_w.transpose(0, 2, 1, 3).reshape(DEPTH, SG_CHUNK, SG_GROUPS * SG_CHUNK).astype(BF16)
    sbias = jnp.repeat(sg_b.transpose(0, 2, 1), SG_W // SG_GROUPS, axis=2)
    grp = np.arange(CV_W) // (CV_W // CV_GROUPS)
    gmat = jnp.asarray((grp[:, None] == grp[None, :]) / (CV_W // CV_GROUPS), BF16)
    fg = final_norm_g.reshape(1, d)

    tm = 512
    assert bn % 2 == 0 and 2 * lc == tm
    bp = bn // 2

    def pair(t):
        return t.reshape((bp, 2 * lc) + t.shape[2:])

    def unpair(t):
        return t.reshape((bn, lc) + t.shape[2:])

    xc = pair(ctx)
    for l in range(DEPTH):
        last = l == DEPTH - 1
        mod_l = mods[l, :bn].reshape(bn, 6, d)
        mod_c = jnp.broadcast_to(mods[l, bn].reshape(1, 6, d), (bp, 6, d))
        g1 = norm1_g[l].reshape(1, d)
        g2 = norm2_g[l].reshape(1, d)
        in_w = (w_in_b, sg_norm_g[l].reshape(1, SG_W), sg_norm_b[l].reshape(1, SG_W), l)
        mix_w = (ws_b[l], sbias[l], cv_w[l], cv_b[l].reshape(1, CV_W),
                 cv_norm_g[l].reshape(1, CV_W), cv_norm_b[l].reshape(1, CV_W), gmat, w_out_b, l)
        ffn_w = (w_up_b, ffn_conv_w[l], ffn_conv_b[l].reshape(1, D_FF), w_dn_b, fg, l)

        qkv_c, vt_c, sg_c, cvh_c = _inproj(xc, mod_c, g1, *in_w, tm, 3)
        qkv_c = unpair(qkv_c)
        vt_c = vt_c.reshape(bn, lc // LANES, ATT_W, LANES)
        qk, vt, sg2, cvh = _inproj(x, mod_l, g1, *in_w, tm, 2)
        att = _attention(qk, vt, qkv_c, vt_c, tables, l)
        x = _mix(att, sg2, cvh, x, mod_l, *mix_w, tm)
        x = _ffn(x, mod_l, g2, *ffn_w, tm, last)
        if not last:
            att_c = pair(_ctx_attention(qkv_c))
            xc = _mix(att_c, sg_c, cvh_c, xc, mod_c, *mix_w, tm, two_seq=True)
            xc = _ffn(xc, mod_c, g2, *ffn_w, tm, False, two_seq=True)
    return x
```

```python
import functools

import jax
import jax.numpy as jnp
import numpy as np
from jax import lax
from jax.experimental import pallas as pl
from jax.experimental.pallas import tpu as pltpu

D_MODEL = 1024
DEPTH = 4
GRID_W = 64
HEAD_DIM = 64
ATT_W = 512
NA_HEADS = 8
NA_ROWS = 8
NA_COLS = 16
SG_W = 256
SG_GROUPS = 4
SG_CHUNK = 128
CV_W = 256
CV_GROUPS = 4
CV_KERNEL = 31
N_IN = 3 * ATT_W + 2 * SG_W + 2 * CV_W
D_FF = 2816
FFN_KERNEL = 3
EPS = 1e-6
NEG_INF = -1e30

LANES = 128
HEAD_PAIRS = ATT_W // LANES
Q_ROWS = 2
K_ROWS = 10
CV_HALO = 16
FFN_HALO = 8
FF_CHUNK = 256
ATT_UNROLL = 16
ATT_AHEAD = 2
LOG2E = 1.4426950408889634
SUM_ROWS = 16
VMEM_LIMIT = 56 * 1024 * 1024

F32 = jnp.float32
BF16 = jnp.bfloat16


def _cparams(sem):
    return pltpu.CompilerParams(dimension_semantics=sem, vmem_limit_bytes=VMEM_LIMIT)


def _full(shape):
    nd = len(shape)
    return pl.BlockSpec(shape, lambda *_: (0,) * nd)


def _layer(shape, layer, **kw):
    nd = len(shape)
    return pl.BlockSpec((None,) + tuple(shape), lambda *_: (layer,) + (0,) * nd, **kw)


def _ada_kernel(c_ref, w_ref, b_ref, o_ref):
    cs = c_ref[...]
    cs = cs * jax.nn.sigmoid(cs)
    o_ref[0] = jnp.dot(cs.astype(BF16), w_ref[0].astype(BF16),
                       preferred_element_type=F32) + b_ref[0]


def _ada_table(c_all, ada_w, ada_b):
    rows = c_all.shape[0]
    tn = 1536
    return pl.pallas_call(
        _ada_kernel,
        grid=(DEPTH, 6 * D_MODEL // tn),
        in_specs=[
            pl.BlockSpec((rows, D_MODEL), lambda l, j: (0, 0)),
            pl.BlockSpec((1, D_MODEL, tn), lambda l, j: (l, 0, j)),
            pl.BlockSpec((1, 1, tn), lambda l, j: (l, 0, j)),
        ],
        out_specs=pl.BlockSpec((1, rows, tn), lambda l, j: (l, 0, j)),
        out_shape=jax.ShapeDtypeStruct((DEPTH, rows, 6 * D_MODEL), F32),
        compiler_params=_cparams(("arbitrary", "arbitrary")),
        name="ada_table",
    )(c_all, ada_w, ada_b.reshape(DEPTH, 1, 6 * D_MODEL))


def _norm_mod(x, g, shift, scale):
    ms = jnp.mean(x * x, axis=-1, keepdims=True)
    return (x * lax.rsqrt(ms + EPS) * g) * (1.0 + scale) + shift


_NT = (((1,), (1,)), ((), ()))


def _inproj_kernel(x_ref, mod_ref, g_ref, w_ref, sgg_ref, sgb_ref,
                   qkv_ref, vt_ref, sg_ref, cvh_ref, *, n_tok):
    tm = x_ref.shape[1]
    h = _norm_mod(x_ref[0], g_ref[...], mod_ref[0, 0:1, :], mod_ref[0, 1:2, :]).astype(BF16)
    z = jnp.dot(h, w_ref[:, 3 * ATT_W:], preferred_element_type=F32)
    v = jnp.dot(h, w_ref[:, 2 * ATT_W:3 * ATT_W], preferred_element_type=F32)
    q = jnp.dot(h, w_ref[:, 0:ATT_W], preferred_element_type=F32)
    k = jnp.dot(h, w_ref[:, ATT_W:2 * ATT_W], preferred_element_type=F32)
    qkv_ref[0, :, 0:ATT_W] = (q * (HEAD_DIM ** -0.5 * LOG2E)).astype(BF16)
    qkv_ref[0, :, ATT_W:2 * ATT_W] = k.astype(BF16)
    if n_tok == 3:
        qkv_ref[0, :, 2 * ATT_W:3 * ATT_W] = v.astype(BF16)
    vt = v.T.astype(BF16)
    for c in range(tm // LANES):
        vt_ref[0, c] = vt[:, c * LANES:(c + 1) * LANES]

    sg_ref[0, :, 0:SG_W] = jax.nn.gelu(z[:, 0:SG_W]).astype(BF16)
    gv = jax.nn.gelu(z[:, SG_W:2 * SG_W])
    mu = jnp.mean(gv, axis=-1, keepdims=True)
    gc = gv - mu
    var = jnp.mean(gc * gc, axis=-1, keepdims=True)
    sg_ref[0, :, SG_W:2 * SG_W] = ((gc * lax.rsqrt(var + EPS)) * sgg_ref[...]
                                   + sgb_ref[...]).astype(BF16)
    a0 = 2 * SG_W
    cvh_ref[0] = z[:, a0:a0 + CV_W] * jax.nn.sigmoid(z[:, a0 + CV_W:a0 + 2 * CV_W])


def _inproj(x, mod, g, w_all, sgg, sgb, layer, tm, n_tok):
    bn, ln, d = x.shape
    nc = tm // LANES
    return pl.pallas_call(
        functools.partial(_inproj_kernel, n_tok=n_tok),
        grid=(bn, ln // tm),
        in_specs=[
            pl.BlockSpec((1, tm, d), lambda b, i: (b, i, 0)),
            pl.BlockSpec((1, 6, d), lambda b, i: (b, 0, 0)),
            _full((1, d)),
            _layer((d, N_IN), layer),
            _full((1, SG_W)), _full((1, SG_W)),
        ],
        out_specs=[
            pl.BlockSpec((1, tm, n_tok * ATT_W), lambda b, i: (b, i, 0)),
            pl.BlockSpec((1, nc, ATT_W, LANES), lambda b, i: (b, i, 0, 0)),
            pl.BlockSpec((1, tm, 2 * SG_W), lambda b, i: (b, i, 0)),
            pl.BlockSpec((1, tm, CV_W), lambda b, i: (b, i, 0)),
        ],
        out_shape=[
            jax.ShapeDtypeStruct((bn, ln, n_tok * ATT_W), BF16),
            jax.ShapeDtypeStruct((bn, ln // LANES, ATT_W, LANES), BF16),
            jax.ShapeDtypeStruct((bn, ln, 2 * SG_W), BF16),
            jax.ShapeDtypeStruct((bn, ln, CV_W), F32),
        ],
        compiler_params=_cparams(("parallel", "parallel")),
        name="inproj",
    )(x, mod, g, w_all, sgg, sgb)


GRID_ROWS = 2048 // GRID_W
RPB_W = 2 * NA_COLS - 1
RPB_PAD = 512
ROW_CFGS = ((0, 0), (2, 0), (8, 4), (GRID_ROWS - 4, GRID_ROWS - K_ROWS),
            (GRID_ROWS - 2, GRID_ROWS - K_ROWS))


def _table_kernel(rpb_ref, o_ref):
    r_base = pl.program_id(0) * NA_HEADS + pl.program_id(1) * 2
    shape = (GRID_W, LANES)
    kc = lax.broadcasted_iota(jnp.int32, shape, 0)
    lane = lax.broadcasted_iota(jnp.int32, shape, 1)
    low = lane < GRID_W
    qc = jnp.where(low, lane, lane - GRID_W)
    off = kc - qc + (NA_COLS - 1)
    sc = jnp.clip(qc - NA_COLS // 2, 0, GRID_W - NA_COLS)
    in_cols = (kc >= sc) & (kc < sc + NA_COLS)
    neg = jnp.full(shape, NEG_INF, F32)
    for hh in range(2):
        toep = []
        for u in range(2 * NA_ROWS - 1):
            acc = jnp.zeros(shape, F32)
            for w in range(RPB_W):
                acc = jnp.where(off == w, rpb_ref[r_base + hh, u * RPB_W + w], acc)
            toep.append(jnp.where(in_cols, acc * LOG2E, neg))
        for cfg, (r0, a) in enumerate(ROW_CFGS):
            for j in range(K_ROWS):
                kr = a + j
                halves = []
                for qr in (r0, r0 + 1):
                    sr = min(max(qr - NA_ROWS // 2, 0), GRID_ROWS - NA_ROWS)
                    inside = sr <= kr < sr + NA_ROWS
                    halves.append(toep[kr - qr + NA_ROWS - 1] if inside else neg)
                o_ref[0, cfg, 0, j * GRID_W:(j + 1) * GRID_W, hh * LANES:(hh + 1) * LANES] = (
                    jnp.where(low, halves[0], halves[1]))


def _bias_tables(na_rpb):
    assert Q_ROWS == 2
    flat = na_rpb.reshape(DEPTH * NA_HEADS, -1)
    flat = jnp.pad(flat, ((0, 0), (0, RPB_PAD - flat.shape[1])))
    nq2, nk = 2 * Q_ROWS * GRID_W, K_ROWS * GRID_W
    return pl.pallas_call(
        _table_kernel,
        grid=(DEPTH, HEAD_PAIRS),
        in_specs=[pl.BlockSpec(memory_space=pltpu.SMEM)],
        out_specs=pl.BlockSpec((1, len(ROW_CFGS), 1, nk, nq2), lambda l, p: (l, 0, p, 0, 0)),
        out_shape=jax.ShapeDtypeStruct((DEPTH, len(ROW_CFGS), HEAD_PAIRS, nk, nq2), F32),
        compiler_params=_cparams(("arbitrary", "arbitrary")),
        name="bias_tables",
    )(flat)


def _stack_heads(q2):
    lane = lax.broadcasted_iota(jnp.int32, q2.shape, 1)
    zero = jnp.zeros_like(q2)
    return jnp.concatenate([jnp.where(lane < HEAD_DIM, q2, zero),
                            jnp.where(lane < HEAD_DIM, zero, q2)], axis=0)


def _unstack_heads(o, n):
    lane = lax.broadcasted_iota(jnp.int32, (n, LANES), 1)
    return jnp.where(lane < HEAD_DIM, o[0:n], o[n:2 * n])


def _attn_kernel(q_ref, k_ref, vt_ref, kc_ref, vct_ref, tab_ref, o_ref):
    nq = Q_ROWS * GRID_W
    nk = K_ROWS * GRID_W
    nrb = q_ref.shape[1] // nq
    rows = q_ref.shape[1] // GRID_W
    row = lax.broadcasted_iota(jnp.int32, (LANES, nq), 0)
    ones_loc = jnp.ones((SUM_ROWS, nk), BF16)
    ones_ctx = jnp.ones((SUM_ROWS, kc_ref.shape[1]), BF16)

    def body(it, carry):
        def place(u):
            rb = it * ATT_UNROLL + u
            c0 = jnp.clip(rb - NA_ROWS // 4, 0, (rows - K_ROWS) // 2)
            cfg = jnp.where(rb < 2, rb, jnp.where(rb < nrb - 2, 2, rb - (nrb - 5)))
            return c0, cfg, pl.multiple_of(rb * nq, nq), pl.multiple_of(c0 * LANES, LANES)

        places = [place(u) for u in range(ATT_UNROLL)]

        def scores(u, p):
            _, cfg, q0, k0 = places[u]
            cols = slice(p * LANES, (p + 1) * LANES)
            qs = _stack_heads(q_ref[0, pl.ds(q0, nq), cols])
            s_loc = lax.dot_general(k_ref[0, pl.ds(k0, nk), cols], qs, _NT,
                                    preferred_element_type=F32) + tab_ref[0, cfg, p]
            s_ctx = lax.dot_general(kc_ref[0, :, cols], qs, _NT, preferred_element_type=F32)
            return s_loc, s_ctx

        steps = [(u, p) for u in range(ATT_UNROLL) for p in range(HEAD_PAIRS)]
        ahead = [scores(*st) for st in steps[:ATT_AHEAD]]
        for i, (u, p) in enumerate(steps):
            c0, _, q0, _ = places[u]
            cols = slice(p * LANES, (p + 1) * LANES)
            s_loc, s_ctx = ahead.pop(0)
            if i + ATT_AHEAD < len(steps):
                ahead.append(scores(*steps[i + ATT_AHEAD]))
            m = jnp.maximum(jnp.max(s_loc, axis=0, keepdims=True),
                            jnp.max(s_ctx, axis=0, keepdims=True))
            p_loc = jnp.exp2(s_loc - m).astype(BF16)
            p_ctx = jnp.exp2(s_ctx - m).astype(BF16)
            vt = jnp.concatenate([vt_ref[0, c0 + c, cols, :] for c in range(nk // LANES)]
                                 , axis=1)
            vct = jnp.concatenate([vct_ref[0, c, cols, :] for c in range(vct_ref.shape[1])],
                                  axis=1)
            o = (jnp.dot(jnp.concatenate([vt, ones_loc], axis=0), p_loc,
                         preferred_element_type=F32)
                 + jnp.dot(jnp.concatenate([vct, ones_ctx], axis=0), p_ctx,
                           preferred_element_type=F32))
            o = o[0:LANES] / o[LANES:LANES + 1]
            o2 = jnp.where(row < HEAD_DIM, o[:, 0:nq], o[:, nq:2 * nq])
            o_ref[0, pl.ds(q0, nq), cols] = o2.T.astype(BF16)
        return carry

    lax.fori_loop(0, nrb // ATT_UNROLL, body, 0)


def _attention(qk, vt, qkv_c, vt_c, tables, layer):
    bn, ln, _ = qk.shape
    lc = qkv_c.shape[1]
    return pl.pallas_call(
        _attn_kernel,
        grid=(bn,),
        in_specs=[
            pl.BlockSpec((1, ln, ATT_W), lambda b: (b, 0, 0)),
            pl.BlockSpec((1, ln, ATT_W), lambda b: (b, 0, 1)),
            pl.BlockSpec((1,) + vt.shape[1:], lambda b: (b, 0, 0, 0)),
            pl.BlockSpec((1, lc, ATT_W), lambda b: (b, 0, 1)),
            pl.BlockSpec((1,) + vt_c.shape[1:], lambda b: (b, 0, 0, 0)),
            pl.BlockSpec((1,) + tables.shape[1:], lambda b: (layer, 0, 0, 0, 0),
                         pipeline_mode=pl.Buffered(1)),
        ],
        out_specs=pl.BlockSpec((1, ln, ATT_W), lambda b: (b, 0, 0)),
        out_shape=jax.ShapeDtypeStruct((bn, ln, ATT_W), BF16),
        compiler_params=_cparams(("parallel",)),
        name="attention",
    )(qk, qk, vt, qkv_c, vt_c, tables)


def _ctx_attn_kernel(q_ref, k_ref, v_ref, o_ref):
    n = q_ref.shape[1]
    for p in range(HEAD_PAIRS):
        cols = slice(p * LANES, (p + 1) * LANES)
        qs = _stack_heads(q_ref[0, :, cols])
        s = lax.dot_general(qs, k_ref[0, :, cols], _NT, preferred_element_type=F32)
        e = jnp.exp2(s - jnp.max(s, axis=-1, keepdims=True))
        den = jnp.sum(e, axis=-1, keepdims=True)
        o = jnp.dot(e.astype(BF16), v_ref[0, :, cols], preferred_element_type=F32)
        o_ref[0, :, cols] = _unstack_heads(o / den, n).astype(BF16)


def _ctx_attention(qkv_c):
    bn, lc, _ = qkv_c.shape
    return pl.pallas_call(
        _ctx_attn_kernel,
        grid=(bn,),
        in_specs=[pl.BlockSpec((1, lc, ATT_W), lambda b, j=j: (b, 0, j)) for j in range(3)],
        out_specs=pl.BlockSpec((1, lc, ATT_W), lambda b: (b, 0, 0)),
        out_shape=jax.ShapeDtypeStruct((bn, lc, ATT_W), BF16),
        compiler_params=_cparams(("parallel",)),
        name="ctx_attention",
    )(qkv_c, qkv_c, qkv_c)


def _mix_kernel(att_ref, sg_ref, h_ref, hp_ref, hn_ref, x_ref, mod_ref, ws_ref,
                sbias_ref, cvw_ref, cvb_ref, cvg_ref, cvnb_ref, gmat_ref, wout_ref,
                o_ref, *, two_seq):
    tm = x_ref.shape[1]
    i = pl.program_id(1)
    nt = pl.num_programs(1)

    grp = lax.broadcasted_iota(jnp.int32, (SG_CHUNK, SG_W), 1) // (SG_W // SG_GROUPS)
    mixed = []
    for c in range(tm // SG_CHUNK):
        vch = sg_ref[0, c * SG_CHUNK:(c + 1) * SG_CHUNK, SG_W:2 * SG_W]
        vbd = jnp.concatenate([jnp.where(grp == g, vch, jnp.zeros_like(vch))
                               for g in range(SG_GROUPS)], axis=0)
        mixed.append(jnp.dot(ws_ref[...], vbd, preferred_element_type=F32) + sbias_ref[...])
    sg = sg_ref[0, :, 0:SG_W].astype(F32) * jnp.concatenate(mixed, axis=0)

    hf = jnp.concatenate([jnp.where(i > 0, hp_ref[0], 0.0), h_ref[0],
                          jnp.where(i < nt - 1, hn_ref[0], 0.0)], axis=0)
    th = tm // 2
    nh = th + 2 * CV_HALO
    zero_halo = jnp.zeros((CV_HALO, CV_W), F32)

    def group_mean(t):
        hi = t.astype(BF16)
        lo = (t - hi.astype(F32)).astype(BF16)
        return (jnp.dot(hi, gmat_ref[...], preferred_element_type=F32)
                + jnp.dot(lo, gmat_ref[...], preferred_element_type=F32))

    halves = [slice(0, th), slice(th, tm)]
    ys, ymean, out = [], [], []
    for rows in halves:
        r0 = rows.start
        if two_seq:
            hh = jnp.concatenate([zero_halo, h_ref[0, rows, :], zero_halo], axis=0)
        else:
            hh = hf[r0:r0 + nh]
        shifted = [hh] + [pltpu.roll(hh, nh - b, axis=0) for b in range(1, 8)]
        y = jnp.zeros((th, CV_W), F32) + cvb_ref[...]
        for k in range(CV_KERNEL):
            s = CV_HALO - CV_KERNEL // 2 + k
            y = y + cvw_ref[k:k + 1, :] * shifted[s % 8][8 * (s // 8):8 * (s // 8) + th]
        ys.append(y)
        ymean.append(group_mean(y))
        out.append(jnp.dot(att_ref[0, rows, :], wout_ref[0:ATT_W, :], preferred_element_type=F32)
                   + jnp.dot(sg[rows].astype(BF16), wout_ref[ATT_W:ATT_W + SG_W, :],
                             preferred_element_type=F32))
    yc = [y - m for y, m in zip(ys, ymean)]
    yvar = [group_mean(t * t) for t in yc]
    for rows, o, t, var in zip(halves, out, yc, yvar):
        yn = t * lax.rsqrt(var + EPS) * cvg_ref[...] + cvnb_ref[...]
        cv = yn * jax.nn.sigmoid(yn)
        o = o + jnp.dot(cv.astype(BF16), wout_ref[ATT_W + SG_W:, :], preferred_element_type=F32)
        o_ref[0, rows, :] = x_ref[0, rows, :] + mod_ref[0, 2:3, :] * o


def _mix(att, sg2, cvh, x, mod, ws, sbias, cvw, cvb, cvg, cvnb, gmat, wout_all, layer, tm,
         two_seq=False):
    bn, ln, d = x.shape
    nh = tm // CV_HALO
    last = ln // CV_HALO - 1
    return pl.pallas_call(
        functools.partial(_mix_kernel, two_seq=two_seq),
        grid=(bn, ln // tm),
        in_specs=[
            pl.BlockSpec((1, tm, ATT_W), lambda b, i: (b, i, 0)),
            pl.BlockSpec((1, tm, 2 * SG_W), lambda b, i: (b, i, 0)),
            pl.BlockSpec((1, tm, CV_W), lambda b, i: (b, i, 0)),
            pl.BlockSpec((1, CV_HALO, CV_W), lambda b, i: (b, jnp.maximum(i * nh - 1, 0), 0)),
            pl.BlockSpec((1, CV_HALO, CV_W), lambda b, i: (b, jnp.minimum((i + 1) * nh, last), 0)),
            pl.BlockSpec((1, tm, d), lambda b, i: (b, i, 0)),
            pl.BlockSpec((1, 6, d), lambda b, i: (b, 0, 0)),
            _full((SG_CHUNK, SG_GROUPS * SG_CHUNK)),
            _full((SG_CHUNK, SG_W)),
            _full((CV_KERNEL, CV_W)), _full((1, CV_W)), _full((1, CV_W)), _full((1, CV_W)),
            _full((CV_W, CV_W)),
            _layer((d, d), layer),
        ],
        out_specs=pl.BlockSpec((1, tm, d), lambda b, i: (b, i, 0)),
        out_shape=jax.ShapeDtypeStruct((bn, ln, d), F32),
        compiler_params=_cparams(("parallel", "parallel")),
        name="mix_out",
    )(att, sg2, cvh, cvh, cvh, x, mod, ws, sbias, cvw, cvb, cvg, cvnb, gmat, wout_all)


def _ffn_kernel(x_ref, xp_ref, xn_ref, mod_ref, g_ref, wup_ref, cw_ref, cb_ref, wdn_ref, fg_ref,
                o_ref, act_scr, *, final, two_seq):
    tm = x_ref.shape[1]
    i = pl.program_id(1)
    nt = pl.num_programs(1)
    shift, scale = mod_ref[0, 3:4, :], mod_ref[0, 4:5, :]
    xm = x_ref[0]
    hm = _norm_mod(xm, g_ref[...], shift, scale).astype(BF16)
    hp = jnp.where(i > 0, _norm_mod(xp_ref[0], g_ref[...], shift, scale), 0.0).astype(BF16)
    hn = jnp.where(i < nt - 1, _norm_mod(xn_ref[0], g_ref[...], shift, scale), 0.0).astype(BF16)
    hh = jnp.concatenate([hp, hm, hn], axis=0)

    n = tm + 2 * FFN_HALO
    row = lax.broadcasted_iota(jnp.int32, (tm, 1), 0)
    for j in range(D_FF // FF_CHUNK):
        cols = slice(j * FF_CHUNK, (j + 1) * FF_CHUNK)
        g = jnp.dot(hh, wup_ref[:, cols], preferred_element_type=F32)
        up = jnp.dot(hm, wup_ref[:, D_FF + j * FF_CHUNK:D_FF + (j + 1) * FF_CHUNK],
                     preferred_element_type=F32)
        g_prev = pltpu.roll(g, 1, axis=0)[FFN_HALO:FFN_HALO + tm]
        g_next = pltpu.roll(g, n - 1, axis=0)[FFN_HALO:FFN_HALO + tm]
        if two_seq:
            g_prev = jnp.where(row == tm // 2, 0.0, g_prev)
            g_next = jnp.where(row == tm // 2 - 1, 0.0, g_next)
        gc = (cb_ref[:, cols] + cw_ref[0:1, cols] * g_prev
              + cw_ref[1:2, cols] * g[FFN_HALO:FFN_HALO + tm] + cw_ref[2:3, cols] * g_next)
        act_scr[:, cols] = (gc * jax.nn.sigmoid(gc) * up).astype(BF16)
    acc = jnp.dot(act_scr[...], wdn_ref[...], preferred_element_type=F32)
    xo = xm + mod_ref[0, 5:6, :] * acc
    if final:
        ms = jnp.mean(xo * xo, axis=-1, keepdims=True)
        xo = xo * lax.rsqrt(ms + EPS) * fg_ref[...]
    o_ref[0] = xo


def _ffn(x, mod, g, wup_all, cw, cb, wdn_all, fg, layer, tm, final, two_seq=False):
    bn, ln, d = x.shape
    nh = tm // FFN_HALO
    last = ln // FFN_HALO - 1
    single = dict(pipeline_mode=pl.Buffered(1))
    return pl.pallas_call(
        functools.partial(_ffn_kernel, final=final, two_seq=two_seq),
        grid=(bn, ln // tm),
        in_specs=[
            pl.BlockSpec((1, tm, d), lambda b, i: (b, i, 0)),
            pl.BlockSpec((1, FFN_HALO, d), lambda b, i: (b, jnp.maximum(i * nh - 1, 0), 0)),
            pl.BlockSpec((1, FFN_HALO, d), lambda b, i: (b, jnp.minimum((i + 1) * nh, last), 0)),
            pl.BlockSpec((1, 6, d), lambda b, i: (b, 0, 0)),
            _full((1, d)),
            _layer((d, 2 * D_FF), layer, **single),
            _full((FFN_KERNEL, D_FF)), _full((1, D_FF)),
            _layer((D_FF, d), layer, **single),
            _full((1, d)),
        ],
        out_specs=pl.BlockSpec((1, tm, d), lambda b, i: (b, i, 0)),
        out_shape=jax.ShapeDtypeStruct((bn, ln, d), F32),
        scratch_shapes=[pltpu.VMEM((tm, D_FF), BF16)],
        compiler_params=_cparams(("parallel", "parallel")),
        name="conv_ffn",
    )(x, x, x, mod, g, wup_all, cw, cb, wdn_all, fg)


def kernel(x, c, ctx, c_ctx, ada_w, ada_b, norm1_g, w_in, na_rpb, sg_norm_g, sg_norm_b, sg_w,
           sg_b, cv_w, cv_b, cv_norm_g, cv_norm_b, w_out, norm2_g, ffn_w_up, ffn_conv_w,
           ffn_conv_b, ffn_w_down, final_norm_g):
    bn, ln, d = x.shape
    lc = ctx.shape[1]
    assert (d, ln, ln // GRID_W) == (D_MODEL, 2048, 32) and lc % SG_CHUNK == 0

    rows = -(-(bn + 1) // 8) * 8
    c_all = jnp.zeros((rows, d), F32).at[:bn].set(c).at[bn].set(c_ctx)
    mods = _ada_table(c_all, ada_w, ada_b)
    tables = _bias_tables(na_rpb)

    w_in_b = w_in.astype(BF16)
    w_out_b = w_out.astype(BF16)
    w_up_b = ffn_w_up.astype(BF16)
    w_dn_b = ffn_w_down.astype(BF16)
    ws_b = sg_w.transpose(0, 2, 1, 3).reshape(DEPTH, SG_CHUNK, SG_GROUPS * SG_CHUNK).astype(BF16)
    sbias = jnp.repeat(sg_b.transpose(0, 2, 1), SG_W // SG_GROUPS, axis=2)
    grp = np.arange(CV_W) // (CV_W // CV_GROUPS)
    gmat = jnp.asarray((grp[:, None] == grp[None, :]) / (CV_W // CV_GROUPS), BF16)
    fg = final_norm_g.reshape(1, d)

    tm = 512
    assert bn % 2 == 0 and 2 * lc == tm
    bp = bn // 2

    def pair(t):
        return t.reshape((bp, 2 * lc) + t.shape[2:])

    def unpair(t):
        return t.reshape((bn, lc) + t.shape[2:])

    xc = pair(ctx)
    for l in range(DEPTH):
        last = l == DEPTH - 1
        mod_l = mods[l, :bn].reshape(bn, 6, d)
        mod_c = jnp.broadcast_to(mods[l, bn].reshape(1, 6, d), (bp, 6, d))
        g1 = norm1_g[l].reshape(1, d)
        g2 = norm2_g[l].reshape(1, d)
        in_w = (w_in_b, sg_norm_g[l].reshape(1, SG_W), sg_norm_b[l].reshape(1, SG_W), l)
        mix_w = (ws_b[l], sbias[l], cv_w[l], cv_b[l].reshape(1, CV_W),
                 cv_norm_g[l].reshape(1, CV_W), cv_norm_b[l].reshape(1, CV_W), gmat, w_out_b, l)
        ffn_w = (w_up_b, ffn_conv_w[l], ffn_conv_b[l].reshape(1, D_FF), w_dn_b, fg, l)

        qkv_c, vt_c, sg_c, cvh_c = _inproj(xc, mod_c, g1, *in_w, tm, 3)
        qkv_c = unpair(qkv_c)
        vt_c = vt_c.reshape(bn, lc // LANES, ATT_W, LANES)
        qk, vt, sg2, cvh = _inproj(x, mod_l, g1, *in_w, 2 * tm, 2)
        att = _attention(qk, vt, qkv_c, vt_c, tables, l)
        x = _mix(att, sg2, cvh, x, mod_l, *mix_w, tm)
        x = _ffn(x, mod_l, g2, *ffn_w, 2 * tm, last)
        if not last:
            att_c = pair(_ctx_attention(qkv_c))
            xc = _mix(att_c, sg_c, cvh_c, xc, mod_c, *mix_w, tm, two_seq=True)
            xc = _ffn(xc, mod_c, g2, *ffn_w, tm, False, two_seq=True)
    return x
```

```python
import functools

import jax
import jax.numpy as jnp
import numpy as np
from jax import lax
from jax.experimental import pallas as pl
from jax.experimental.pallas import tpu as pltpu

D_MODEL = 1024
DEPTH = 4
GRID_W = 64
HEAD_DIM = 64
ATT_W = 512
NA_HEADS = 8
NA_ROWS = 8
NA_COLS = 16
SG_W = 256
SG_GROUPS = 4
SG_CHUNK = 128
CV_W = 256
CV_GROUPS = 4
CV_KERNEL = 31
N_IN = 3 * ATT_W + 2 * SG_W + 2 * CV_W
D_FF = 2816
FFN_KERNEL = 3
EPS = 1e-6
NEG_INF = -1e30

LANES = 128
HEAD_PAIRS = ATT_W // LANES
Q_ROWS = 2
K_ROWS = 10
CV_HALO = 16
FFN_HALO = 8
FF_CHUNK = 256
ATT_UNROLL = 16
ATT_AHEAD = 2
LOG2E = 1.4426950408889634
SUM_ROWS = 16
VMEM_LIMIT = 56 * 1024 * 1024

F32 = jnp.float32
BF16 = jnp.bfloat16


def _cparams(sem):
    return pltpu.CompilerParams(dimension_semantics=sem, vmem_limit_bytes=VMEM_LIMIT)


def _full(shape):
    nd = len(shape)
    return pl.BlockSpec(shape, lambda *_: (0,) * nd)


def _layer(shape, layer, **kw):
    nd = len(shape)
    return pl.BlockSpec((None,) + tuple(shape), lambda *_: (layer,) + (0,) * nd, **kw)


def _ada_kernel(c_ref, w_ref, b_ref, o_ref):
    cs = c_ref[...]
    cs = cs * jax.nn.sigmoid(cs)
    o_ref[0] = jnp.dot(cs.astype(BF16), w_ref[0].astype(BF16),
                       preferred_element_type=F32) + b_ref[0]


def _ada_table(c_all, ada_w, ada_b):
    rows = c_all.shape[0]
    tn = 1536
    return pl.pallas_call(
        _ada_kernel,
        grid=(DEPTH, 6 * D_MODEL // tn),
        in_specs=[
            pl.BlockSpec((rows, D_MODEL), lambda l, j: (0, 0)),
            pl.BlockSpec((1, D_MODEL, tn), lambda l, j: (l, 0, j)),
            pl.BlockSpec((1, 1, tn), lambda l, j: (l, 0, j)),
        ],
        out_specs=pl.BlockSpec((1, rows, tn), lambda l, j: (l, 0, j)),
        out_shape=jax.ShapeDtypeStruct((DEPTH, rows, 6 * D_MODEL), F32),
        compiler_params=_cparams(("arbitrary", "arbitrary")),
        name="ada_table",
    )(c_all, ada_w, ada_b.reshape(DEPTH, 1, 6 * D_MODEL))


def _norm_mod(x, g, shift, scale):
    ms = jnp.mean(x * x, axis=-1, keepdims=True)
    return (x * lax.rsqrt(ms + EPS) * g) * (1.0 + scale) + shift


_NT = (((1,), (1,)), ((), ()))


def _inproj_kernel(x_ref, mod_ref, g_ref, w_ref, sgg_ref, sgb_ref,
                   qkv_ref, vt_ref, sg_ref, cvh_ref, *, n_tok):
    tm = x_ref.shape[1]
    h = _norm_mod(x_ref[0], g_ref[...], mod_ref[0, 0:1, :], mod_ref[0, 1:2, :]).astype(BF16)
    z = jnp.dot(h, w_ref[:, 3 * ATT_W:], preferred_element_type=F32)
    v = jnp.dot(h, w_ref[:, 2 * ATT_W:3 * ATT_W], preferred_element_type=F32)
    q = jnp.dot(h, w_ref[:, 0:ATT_W], preferred_element_type=F32)
    k = jnp.dot(h, w_ref[:, ATT_W:2 * ATT_W], preferred_element_type=F32)
    qkv_ref[0, :, 0:ATT_W] = (q * (HEAD_DIM ** -0.5 * LOG2E)).astype(BF16)
    qkv_ref[0, :, ATT_W:2 * ATT_W] = k.astype(BF16)
    if n_tok == 3:
        qkv_ref[0, :, 2 * ATT_W:3 * ATT_W] = v.astype(BF16)
    vt = v.T.astype(BF16)
    for c in range(tm // LANES):
        vt_ref[0, c] = vt[:, c * LANES:(c + 1) * LANES]

    sg_ref[0, :, 0:SG_W] = jax.nn.gelu(z[:, 0:SG_W]).astype(BF16)
    gv = jax.nn.gelu(z[:, SG_W:2 * SG_W])
    mu = jnp.mean(gv, axis=-1, keepdims=True)
    gc = gv - mu
    var = jnp.mean(gc * gc, axis=-1, keepdims=True)
    sg_ref[0, :, SG_W:2 * SG_W] = ((gc * lax.rsqrt(var + EPS)) * sgg_ref[...]
                                   + sgb_ref[...]).astype(BF16)
    a0 = 2 * SG_W
    cvh_ref[0] = z[:, a0:a0 + CV_W] * jax.nn.sigmoid(z[:, a0 + CV_W:a0 + 2 * CV_W])


def _inproj(x, mod, g, w_all, sgg, sgb, layer, tm, n_tok):
    bn, ln, d = x.shape
    nc = tm // LANES
    return pl.pallas_call(
        functools.partial(_inproj_kernel, n_tok=n_tok),
        grid=(bn, ln // tm),
        in_specs=[
            pl.BlockSpec((1, tm, d), lambda b, i: (b, i, 0)),
            pl.BlockSpec((1, 6, d), lambda b, i: (b, 0, 0)),
            _full((1, d)),
            _layer((d, N_IN), layer),
            _full((1, SG_W)), _full((1, SG_W)),
        ],
        out_specs=[
            pl.BlockSpec((1, tm, n_tok * ATT_W), lambda b, i: (b, i, 0)),
            pl.BlockSpec((1, nc, ATT_W, LANES), lambda b, i: (b, i, 0, 0)),
            pl.BlockSpec((1, tm, 2 * SG_W), lambda b, i: (b, i, 0)),
            pl.BlockSpec((1, tm, CV_W), lambda b, i: (b, i, 0)),
        ],
        out_shape=[
            jax.ShapeDtypeStruct((bn, ln, n_tok * ATT_W), BF16),
            jax.ShapeDtypeStruct((bn, ln // LANES, ATT_W, LANES), BF16),
            jax.ShapeDtypeStruct((bn, ln, 2 * SG_W), BF16),
            jax.ShapeDtypeStruct((bn, ln, CV_W), F32),
        ],
        compiler_params=_cparams(("parallel", "parallel")),
        name="inproj",
    )(x, mod, g, w_all, sgg, sgb)


GRID_ROWS = 2048 // GRID_W
RPB_W = 2 * NA_COLS - 1
RPB_PAD = 512
ROW_CFGS = ((0, 0), (2, 0), (8, 4), (GRID_ROWS - 4, GRID_ROWS - K_ROWS),
            (GRID_ROWS - 2, GRID_ROWS - K_ROWS))


def _table_kernel(rpb_ref, o_ref):
    r_base = pl.program_id(0) * NA_HEADS + pl.program_id(1) * 2
    shape = (GRID_W, LANES)
    kc = lax.broadcasted_iota(jnp.int32, shape, 0)
    lane = lax.broadcasted_iota(jnp.int32, shape, 1)
    low = lane < GRID_W
    qc = jnp.where(low, lane, lane - GRID_W)
    off = kc - qc + (NA_COLS - 1)
    sc = jnp.clip(qc - NA_COLS // 2, 0, GRID_W - NA_COLS)
    in_cols = (kc >= sc) & (kc < sc + NA_COLS)
    neg = jnp.full(shape, NEG_INF, F32)
    for hh in range(2):
        toep = []
        for u in range(2 * NA_ROWS - 1):
            acc = jnp.zeros(shape, F32)
            for w in range(RPB_W):
                acc = jnp.where(off == w, rpb_ref[r_base + hh, u * RPB_W + w], acc)
            toep.append(jnp.where(in_cols, acc * LOG2E, neg))
        for cfg, (r0, a) in enumerate(ROW_CFGS):
            for j in range(K_ROWS):
                kr = a + j
                halves = []
                for qr in (r0, r0 + 1):
                    sr = min(max(qr - NA_ROWS // 2, 0), GRID_ROWS - NA_ROWS)
                    inside = sr <= kr < sr + NA_ROWS
                    halves.append(toep[kr - qr + NA_ROWS - 1] if inside else neg)
                o_ref[0, cfg, 0, j * GRID_W:(j + 1) * GRID_W, hh * LANES:(hh + 1) * LANES] = (
                    jnp.where(low, halves[0], halves[1]))


def _bias_tables(na_rpb):
    assert Q_ROWS == 2
    flat = na_rpb.reshape(DEPTH * NA_HEADS, -1)
    flat = jnp.pad(flat, ((0, 0), (0, RPB_PAD - flat.shape[1])))
    nq2, nk = 2 * Q_ROWS * GRID_W, K_ROWS * GRID_W
    return pl.pallas_call(
        _table_kernel,
        grid=(DEPTH, HEAD_PAIRS),
        in_specs=[pl.BlockSpec(memory_space=pltpu.SMEM)],
        out_specs=pl.BlockSpec((1, len(ROW_CFGS), 1, nk, nq2), lambda l, p: (l, 0, p, 0, 0)),
        out_shape=jax.ShapeDtypeStruct((DEPTH, len(ROW_CFGS), HEAD_PAIRS, nk, nq2), F32),
        compiler_params=_cparams(("arbitrary", "arbitrary")),
        name="bias_tables",
    )(flat)


def _stack_heads(q2):
    lane = lax.broadcasted_iota(jnp.int32, q2.shape, 1)
    zero = jnp.zeros_like(q2)
    return jnp.concatenate([jnp.where(lane < HEAD_DIM, q2, zero),
                            jnp.where(lane < HEAD_DIM, zero, q2)], axis=0)


def _unstack_heads(o, n):
    lane = lax.broadcasted_iota(jnp.int32, (n, LANES), 1)
    return jnp.where(lane < HEAD_DIM, o[0:n], o[n:2 * n])


def _attn_kernel(q_ref, k_ref, vt_ref, kc_ref, vct_ref, tab_ref, o_ref):
    nq = Q_ROWS * GRID_W
    nk = K_ROWS * GRID_W
    nrb = q_ref.shape[1] // nq
    rows = q_ref.shape[1] // GRID_W
    row = lax.broadcasted_iota(jnp.int32, (LANES, nq), 0)
    ones_loc = jnp.ones((SUM_ROWS, nk), BF16)
    ones_ctx = jnp.ones((SUM_ROWS, kc_ref.shape[1]), BF16)

    def body(it, carry):
        def place(u):
            rb = it * ATT_UNROLL + u
            c0 = jnp.clip(rb - NA_ROWS // 4, 0, (rows - K_ROWS) // 2)
            cfg = jnp.where(rb < 2, rb, jnp.where(rb < nrb - 2, 2, rb - (nrb - 5)))
            return c0, cfg, pl.multiple_of(rb * nq, nq), pl.multiple_of(c0 * LANES, LANES)

        places = [place(u) for u in range(ATT_UNROLL)]

        def scores(u, p):
            _, cfg, q0, k0 = places[u]
            cols = slice(p * LANES, (p + 1) * LANES)
            qs = _stack_heads(q_ref[0, pl.ds(q0, nq), cols])
            s_loc = lax.dot_general(k_ref[0, pl.ds(k0, nk), cols], qs, _NT,
                                    preferred_element_type=F32) + tab_ref[0, cfg, p]
            s_ctx = lax.dot_general(kc_ref[0, :, cols], qs, _NT, preferred_element_type=F32)
            return s_loc, s_ctx

        steps = [(u, p) for u in range(ATT_UNROLL) for p in range(HEAD_PAIRS)]
        ahead = [scores(*st) for st in steps[:ATT_AHEAD]]
        for i, (u, p) in enumerate(steps):
            c0, _, q0, _ = places[u]
            cols = slice(p * LANES, (p + 1) * LANES)
            s_loc, s_ctx = ahead.pop(0)
            if i + ATT_AHEAD < len(steps):
                ahead.append(scores(*steps[i + ATT_AHEAD]))
            m = jnp.maximum(jnp.max(s_loc, axis=0, keepdims=True),
                            jnp.max(s_ctx, axis=0, keepdims=True))
            p_loc = jnp.exp2(s_loc - m).astype(BF16)
            p_ctx = jnp.exp2(s_ctx - m).astype(BF16)
            vt = jnp.concatenate([vt_ref[0, c0 + c, cols, :] for c in range(nk // LANES)]
                                 , axis=1)
            vct = jnp.concatenate([vct_ref[0, c, cols, :] for c in range(vct_ref.shape[1])],
                                  axis=1)
            o = (jnp.dot(jnp.concatenate([vt, ones_loc], axis=0), p_loc,
                         preferred_element_type=F32)
                 + jnp.dot(jnp.concatenate([vct, ones_ctx], axis=0), p_ctx,
                           preferred_element_type=F32))
            o = o[0:LANES] / o[LANES:LANES + 1]
            o2 = jnp.where(row < HEAD_DIM, o[:, 0:nq], o[:, nq:2 * nq])
            o_ref[0, pl.ds(q0, nq), cols] = o2.T.astype(BF16)
        return carry

    lax.fori_loop(0, nrb // ATT_UNROLL, body, 0)


def _attention(qk, vt, qkv_c, vt_c, tables, layer):
    bn, ln, _ = qk.shape
    lc = qkv_c.shape[1]
    return pl.pallas_call(
        _attn_kernel,
        grid=(bn,),
        in_specs=[
            pl.BlockSpec((1, ln, ATT_W), lambda b: (b, 0, 0)),
            pl.BlockSpec((1, ln, ATT_W), lambda b: (b, 0, 1)),
            pl.BlockSpec((1,) + vt.shape[1:], lambda b: (b, 0, 0, 0)),
            pl.BlockSpec((1, lc, ATT_W), lambda b: (b, 0, 1)),
            pl.BlockSpec((1,) + vt_c.shape[1:], lambda b: (b, 0, 0, 0)),
            pl.BlockSpec((1,) + tables.shape[1:], lambda b: (layer, 0, 0, 0, 0),
                         pipeline_mode=pl.Buffered(1)),
        ],
        out_specs=pl.BlockSpec((1, ln, ATT_W), lambda b: (b, 0, 0)),
        out_shape=jax.ShapeDtypeStruct((bn, ln, ATT_W), BF16),
        compiler_params=_cparams(("parallel",)),
        name="attention",
    )(qk, qk, vt, qkv_c, vt_c, tables)


def _ctx_attn_kernel(q_ref, k_ref, v_ref, o_ref):
    n = q_ref.shape[1]
    for p in range(HEAD_PAIRS):
        cols = slice(p * LANES, (p + 1) * LANES)
        qs = _stack_heads(q_ref[0, :, cols])
        s = lax.dot_general(qs, k_ref[0, :, cols], _NT, preferred_element_type=F32)
        e = jnp.exp2(s - jnp.max(s, axis=-1, keepdims=True))
        den = jnp.sum(e, axis=-1, keepdims=True)
        o = jnp.dot(e.astype(BF16), v_ref[0, :, cols], preferred_element_type=F32)
        o_ref[0, :, cols] = _unstack_heads(o / den, n).astype(BF16)


def _ctx_attention(qkv_c):
    bn, lc, _ = qkv_c.shape
    return pl.pallas_call(
        _ctx_attn_kernel,
        grid=(bn,),
        in_specs=[pl.BlockSpec((1, lc, ATT_W), lambda b, j=j: (b, 0, j)) for j in range(3)],
        out_specs=pl.BlockSpec((1, lc, ATT_W), lambda b: (b, 0, 0)),
        out_shape=jax.ShapeDtypeStruct((bn, lc, ATT_W), BF16),
        compiler_params=_cparams(("parallel",)),
        name="ctx_attention",
    )(qkv_c, qkv_c, qkv_c)


def _mix_kernel(att_ref, sg_ref, h_ref, hp_ref, hn_ref, x_ref, mod_ref, ws_ref,
                sbias_ref, cvw_ref, cvb_ref, cvg_ref, cvnb_ref, gmat_ref, wout_ref,
                o_ref, y_scr, *, two_seq):
    tm = x_ref.shape[1]
    i = pl.program_id(1)
    nt = pl.num_programs(1)

    grp = lax.broadcasted_iota(jnp.int32, (SG_CHUNK, SG_W), 1) // (SG_W // SG_GROUPS)
    mixed = []
    for c in range(tm // SG_CHUNK):
        vch = sg_ref[0, c * SG_CHUNK:(c + 1) * SG_CHUNK, SG_W:2 * SG_W]
        vbd = jnp.concatenate([jnp.where(grp == g, vch, jnp.zeros_like(vch))
                               for g in range(SG_GROUPS)], axis=0)
        mixed.append(jnp.dot(ws_ref[...], vbd, preferred_element_type=F32) + sbias_ref[...])
    sg = sg_ref[0, :, 0:SG_W].astype(F32) * jnp.concatenate(mixed, axis=0)

    hf = jnp.concatenate([jnp.where(i > 0, hp_ref[0], 0.0), h_ref[0],
                          jnp.where(i < nt - 1, hn_ref[0], 0.0)], axis=0)
    th = tm // 2
    nh = th + 2 * CV_HALO
    zero_halo = jnp.zeros((CV_HALO, CV_W), F32)

    def group_mean(t):
        hi = t.astype(BF16)
        lo = (t - hi.astype(F32)).astype(BF16)
        return (jnp.dot(hi, gmat_ref[...], preferred_element_type=F32)
                + jnp.dot(lo, gmat_ref[...], preferred_element_type=F32))

    halves = [slice(0, th), slice(th, tm)]
    ys, ymean, out = [], [], []
    for rows in halves:
        r0 = rows.start
        if two_seq:
            hh = jnp.concatenate([zero_halo, h_ref[0, rows, :], zero_halo], axis=0)
        else:
            hh = hf[r0:r0 + nh]
        shifted = [hh] + [pltpu.roll(hh, nh - b, axis=0) for b in range(1, 8)]
        for sb in range(0, th, SG_CHUNK):
            yb = jnp.zeros((SG_CHUNK, CV_W), F32) + cvb_ref[...]
            for k in range(CV_KERNEL):
                s = CV_HALO - CV_KERNEL // 2 + k
                lo = sb + 8 * (s // 8)
                yb = yb + cvw_ref[k:k + 1, :] * shifted[s % 8][lo:lo + SG_CHUNK]
            y_scr[r0 + sb:r0 + sb + SG_CHUNK, :] = yb
        y = y_scr[rows, :]
        ys.append(y)
        ymean.append(group_mean(y))
        out.append(jnp.dot(att_ref[0, rows, :], wout_ref[0:ATT_W, :], preferred_element_type=F32)
                   + jnp.dot(sg[rows].astype(BF16), wout_ref[ATT_W:ATT_W + SG_W, :],
                             preferred_element_type=F32))
    yc = [y - m for y, m in zip(ys, ymean)]
    yvar = [group_mean(t * t) for t in yc]
    for rows, o, t, var in zip(halves, out, yc, yvar):
        yn = t * lax.rsqrt(var + EPS) * cvg_ref[...] + cvnb_ref[...]
        cv = yn * jax.nn.sigmoid(yn)
        o = o + jnp.dot(cv.astype(BF16), wout_ref[ATT_W + SG_W:, :], preferred_element_type=F32)
        o_ref[0, rows, :] = x_ref[0, rows, :] + mod_ref[0, 2:3, :] * o


def _mix(att, sg2, cvh, x, mod, ws, sbias, cvw, cvb, cvg, cvnb, gmat, wout_all, layer, tm,
         two_seq=False):
    bn, ln, d = x.shape
    nh = tm // CV_HALO
    last = ln // CV_HALO - 1
    return pl.pallas_call(
        functools.partial(_mix_kernel, two_seq=two_seq),
        grid=(bn, ln // tm),
        in_specs=[
            pl.BlockSpec((1, tm, ATT_W), lambda b, i: (b, i, 0)),
            pl.BlockSpec((1, tm, 2 * SG_W), lambda b, i: (b, i, 0)),
            pl.BlockSpec((1, tm, CV_W), lambda b, i: (b, i, 0)),
            pl.BlockSpec((1, CV_HALO, CV_W), lambda b, i: (b, jnp.maximum(i * nh - 1, 0), 0)),
            pl.BlockSpec((1, CV_HALO, CV_W), lambda b, i: (b, jnp.minimum((i + 1) * nh, last), 0)),
            pl.BlockSpec((1, tm, d), lambda b, i: (b, i, 0)),
            pl.BlockSpec((1, 6, d), lambda b, i: (b, 0, 0)),
            _full((SG_CHUNK, SG_GROUPS * SG_CHUNK)),
            _full((SG_CHUNK, SG_W)),
            _full((CV_KERNEL, CV_W)), _full((1, CV_W)), _full((1, CV_W)), _full((1, CV_W)),
            _full((CV_W, CV_W)),
            _layer((d, d), layer),
        ],
        out_specs=pl.BlockSpec((1, tm, d), lambda b, i: (b, i, 0)),
        out_shape=jax.ShapeDtypeStruct((bn, ln, d), F32),
        scratch_shapes=[pltpu.VMEM((tm, CV_W), F32)],
        compiler_params=_cparams(("parallel", "parallel")),
        name="mix_out",
    )(att, sg2, cvh, cvh, cvh, x, mod, ws, sbias, cvw, cvb, cvg, cvnb, gmat, wout_all)


def _ffn_kernel(x_ref, xp_ref, xn_ref, mod_ref, g_ref, wup_ref, cw_ref, cb_ref, wdn_ref, fg_ref,
                o_ref, act_scr, *, final, two_seq):
    tm = x_ref.shape[1]
    i = pl.program_id(1)
    nt = pl.num_programs(1)
    shift, scale = mod_ref[0, 3:4, :], mod_ref[0, 4:5, :]
    xm = x_ref[0]
    hm = _norm_mod(xm, g_ref[...], shift, scale).astype(BF16)
    hp = jnp.where(i > 0, _norm_mod(xp_ref[0], g_ref[...], shift, scale), 0.0).astype(BF16)
    hn = jnp.where(i < nt - 1, _norm_mod(xn_ref[0], g_ref[...], shift, scale), 0.0).astype(BF16)
    hh = jnp.concatenate([hp, hm, hn], axis=0)

    n = tm + 2 * FFN_HALO
    row = lax.broadcasted_iota(jnp.int32, (tm, 1), 0)
    for j in range(D_FF // FF_CHUNK):
        cols = slice(j * FF_CHUNK, (j + 1) * FF_CHUNK)
        g = jnp.dot(hh, wup_ref[:, cols], preferred_element_type=F32)
        up = jnp.dot(hm, wup_ref[:, D_FF + j * FF_CHUNK:D_FF + (j + 1) * FF_CHUNK],
                     preferred_element_type=F32)
        g_prev = pltpu.roll(g, 1, axis=0)[FFN_HALO:FFN_HALO + tm]
        g_next = pltpu.roll(g, n - 1, axis=0)[FFN_HALO:FFN_HALO + tm]
        if two_seq:
            g_prev = jnp.where(row == tm // 2, 0.0, g_prev)
            g_next = jnp.where(row == tm // 2 - 1, 0.0, g_next)
        gc = (cb_ref[:, cols] + cw_ref[0:1, cols] * g_prev
              + cw_ref[1:2, cols] * g[FFN_HALO:FFN_HALO + tm] + cw_ref[2:3, cols] * g_next)
        act_scr[:, cols] = (gc * jax.nn.sigmoid(gc) * up).astype(BF16)
    acc = jnp.dot(act_scr[...], wdn_ref[...], preferred_element_type=F32)
    xo = xm + mod_ref[0, 5:6, :] * acc
    if final:
        ms = jnp.mean(xo * xo, axis=-1, keepdims=True)
        xo = xo * lax.rsqrt(ms + EPS) * fg_ref[...]
    o_ref[0] = xo


def _ffn(x, mod, g, wup_all, cw, cb, wdn_all, fg, layer, tm, final, two_seq=False):
    bn, ln, d = x.shape
    nh = tm // FFN_HALO
    last = ln // FFN_HALO - 1
    single = dict(pipeline_mode=pl.Buffered(1))
    return pl.pallas_call(
        functools.partial(_ffn_kernel, final=final, two_seq=two_seq),
        grid=(bn, ln // tm),
        in_specs=[
            pl.BlockSpec((1, tm, d), lambda b, i: (b, i, 0)),
            pl.BlockSpec((1, FFN_HALO, d), lambda b, i: (b, jnp.maximum(i * nh - 1, 0), 0)),
            pl.BlockSpec((1, FFN_HALO, d), lambda b, i: (b, jnp.minimum((i + 1) * nh, last), 0)),
            pl.BlockSpec((1, 6, d), lambda b, i: (b, 0, 0)),
            _full((1, d)),
            _layer((d, 2 * D_FF), layer, **single),
            _full((FFN_KERNEL, D_FF)), _full((1, D_FF)),
            _layer((D_FF, d), layer, **single),
            _full((1, d)),
        ],
        out_specs=pl.BlockSpec((1, tm, d), lambda b, i: (b, i, 0)),
        out_shape=jax.ShapeDtypeStruct((bn, ln, d), F32),
        scratch_shapes=[pltpu.VMEM((tm, D_FF), BF16)],
        compiler_params=_cparams(("parallel", "parallel")),
        name="conv_ffn",
    )(x, x, x, mod, g, wup_all, cw, cb, wdn_all, fg)


def kernel(x, c, ctx, c_ctx, ada_w, ada_b, norm1_g, w_in, na_rpb, sg_norm_g, sg_norm_b, sg_w,
           sg_b, cv_w, cv_b, cv_norm_g, cv_norm_b, w_out, norm2_g, ffn_w_up, ffn_conv_w,
           ffn_conv_b, ffn_w_down, final_norm_g):
    bn, ln, d = x.shape
    lc = ctx.shape[1]
    assert (d, ln, ln // GRID_W) == (D_MODEL, 2048, 32) and lc % SG_CHUNK == 0

    rows = -(-(bn + 1) // 8) * 8
    c_all = jnp.zeros((rows, d), F32).at[:bn].set(c).at[bn].set(c_ctx)
    mods = _ada_table(c_all, ada_w, ada_b)
    tables = _bias_tables(na_rpb)

    w_in_b = w_in.astype(BF16)
    w_out_b = w_out.astype(BF16)
    w_up_b = ffn_w_up.astype(BF16)
    w_dn_b = ffn_w_down.astype(BF16)
    ws_b = sg_w.transpose(0, 2, 1, 3).reshape(DEPTH, SG_CHUNK, SG_GROUPS * SG_CHUNK).astype(BF16)
    sbias = jnp.repeat(sg_b.transpose(0, 2, 1), SG_W // SG_GROUPS, axis=2)
    grp = np.arange(CV_W) // (CV_W // CV_GROUPS)
    gmat = jnp.asarray((grp[:, None] == grp[None, :]) / (CV_W // CV_GROUPS), BF16)
    fg = final_norm_g.reshape(1, d)

    tm = 512
    assert bn % 2 == 0 and 2 * lc == tm
    bp = bn // 2

    def pair(t):
        return t.reshape((bp, 2 * lc) + t.shape[2:])

    def unpair(t):
        return t.reshape((bn, lc) + t.shape[2:])

    xc = pair(ctx)
    for l in range(DEPTH):
        last = l == DEPTH - 1
        mod_l = mods[l, :bn].reshape(bn, 6, d)
        mod_c = jnp.broadcast_to(mods[l, bn].reshape(1, 6, d), (bp, 6, d))
        g1 = norm1_g[l].reshape(1, d)
        g2 = norm2_g[l].reshape(1, d)
        in_w = (w_in_b, sg_norm_g[l].reshape(1, SG_W), sg_norm_b[l].reshape(1, SG_W), l)
        mix_w = (ws_b[l], sbias[l], cv_w[l], cv_b[l].reshape(1, CV_W),
                 cv_norm_g[l].reshape(1, CV_W), cv_norm_b[l].reshape(1, CV_W), gmat, w_out_b, l)
        ffn_w = (w_up_b, ffn_conv_w[l], ffn_conv_b[l].reshape(1, D_FF), w_dn_b, fg, l)

        qkv_c, vt_c, sg_c, cvh_c = _inproj(xc, mod_c, g1, *in_w, tm, 3)
        qkv_c = unpair(qkv_c)
        vt_c = vt_c.reshape(bn, lc // LANES, ATT_W, LANES)
        qk, vt, sg2, cvh = _inproj(x, mod_l, g1, *in_w, 2 * tm, 2)
        att = _attention(qk, vt, qkv_c, vt_c, tables, l)
        x = _mix(att, sg2, cvh, x, mod_l, *mix_w, tm)
        x = _ffn(x, mod_l, g2, *ffn_w, 2 * tm, last)
        if not last:
            att_c = pair(_ctx_attention(qkv_c))
            xc = _mix(att_c, sg_c, cvh_c, xc, mod_c, *mix_w, tm, two_seq=True)
            xc = _ffn(xc, mod_c, g2, *ffn_w, tm, False, two_seq=True)
    return x
```

```python
import functools

import jax
import jax.numpy as jnp
import numpy as np
from jax import lax
from jax.experimental import pallas as pl
from jax.experimental.pallas import tpu as pltpu

D_MODEL = 1024
DEPTH = 4
GRID_W = 64
HEAD_DIM = 64
ATT_W = 512
NA_HEADS = 8
NA_ROWS = 8
NA_COLS = 16
SG_W = 256
SG_GROUPS = 4
SG_CHUNK = 128
CV_W = 256
CV_GROUPS = 4
CV_KERNEL = 31
N_IN = 3 * ATT_W + 2 * SG_W + 2 * CV_W
D_FF = 2816
FFN_KERNEL = 3
EPS = 1e-6
NEG_INF = -1e30

LANES = 128
HEAD_PAIRS = ATT_W // LANES
Q_ROWS = 2
K_ROWS = 10
CV_HALO = 16
FFN_HALO = 8
FF_CHUNK = 256
ATT_UNROLL = 16
ATT_AHEAD = 2
LOG2E = 1.4426950408889634
SUM_ROWS = 16
VMEM_LIMIT = 56 * 1024 * 1024

F32 = jnp.float32
BF16 = jnp.bfloat16


def _cparams(sem):
    return pltpu.CompilerParams(dimension_semantics=sem, vmem_limit_bytes=VMEM_LIMIT)


def _full(shape):
    nd = len(shape)
    return pl.BlockSpec(shape, lambda *_: (0,) * nd)


def _layer(shape, layer, **kw):
    nd = len(shape)
    return pl.BlockSpec((None,) + tuple(shape), lambda *_: (layer,) + (0,) * nd, **kw)


def _ada_kernel(c_ref, w_ref, b_ref, o_ref):
    cs = c_ref[...]
    cs = cs * jax.nn.sigmoid(cs)
    o_ref[0] = jnp.dot(cs.astype(BF16), w_ref[0].astype(BF16),
                       preferred_element_type=F32) + b_ref[0]


def _ada_table(c_all, ada_w, ada_b):
    rows = c_all.shape[0]
    tn = 1536
    return pl.pallas_call(
        _ada_kernel,
        grid=(DEPTH, 6 * D_MODEL // tn),
        in_specs=[
            pl.BlockSpec((rows, D_MODEL), lambda l, j: (0, 0)),
            pl.BlockSpec((1, D_MODEL, tn), lambda l, j: (l, 0, j)),
            pl.BlockSpec((1, 1, tn), lambda l, j: (l, 0, j)),
        ],
        out_specs=pl.BlockSpec((1, rows, tn), lambda l, j: (l, 0, j)),
        out_shape=jax.ShapeDtypeStruct((DEPTH, rows, 6 * D_MODEL), F32),
        compiler_params=_cparams(("arbitrary", "arbitrary")),
        name="ada_table",
    )(c_all, ada_w, ada_b.reshape(DEPTH, 1, 6 * D_MODEL))


def _norm_mod(x, g, shift, scale):
    ms = jnp.mean(x * x, axis=-1, keepdims=True)
    return (x * lax.rsqrt(ms + EPS) * g) * (1.0 + scale) + shift


_NT = (((1,), (1,)), ((), ()))


def _inproj_kernel(x_ref, mod_ref, g_ref, w_ref, sgg_ref, sgb_ref,
                   qkv_ref, vt_ref, sg_ref, cvh_ref, *, n_tok):
    tm = x_ref.shape[1]
    h = _norm_mod(x_ref[0], g_ref[...], mod_ref[0, 0:1, :], mod_ref[0, 1:2, :]).astype(BF16)
    z = jnp.dot(h, w_ref[:, 3 * ATT_W:], preferred_element_type=F32)
    v = jnp.dot(h, w_ref[:, 2 * ATT_W:3 * ATT_W], preferred_element_type=F32)
    q = jnp.dot(h, w_ref[:, 0:ATT_W], preferred_element_type=F32)
    k = jnp.dot(h, w_ref[:, ATT_W:2 * ATT_W], preferred_element_type=F32)
    qkv_ref[0, :, 0:ATT_W] = (q * (HEAD_DIM ** -0.5 * LOG2E)).astype(BF16)
    qkv_ref[0, :, ATT_W:2 * ATT_W] = k.astype(BF16)
    if n_tok == 3:
        qkv_ref[0, :, 2 * ATT_W:3 * ATT_W] = v.astype(BF16)
    vt = v.T.astype(BF16)
    for c in range(tm // LANES):
        vt_ref[0, c] = vt[:, c * LANES:(c + 1) * LANES]

    sg_ref[0, :, 0:SG_W] = jax.nn.gelu(z[:, 0:SG_W]).astype(BF16)
    gv = jax.nn.gelu(z[:, SG_W:2 * SG_W])
    mu = jnp.mean(gv, axis=-1, keepdims=True)
    gc = gv - mu
    var = jnp.mean(gc * gc, axis=-1, keepdims=True)
    sg_ref[0, :, SG_W:2 * SG_W] = ((gc * lax.rsqrt(var + EPS)) * sgg_ref[...]
                                   + sgb_ref[...]).astype(BF16)
    a0 = 2 * SG_W
    cvh_ref[0] = z[:, a0:a0 + CV_W] * jax.nn.sigmoid(z[:, a0 + CV_W:a0 + 2 * CV_W])


def _inproj(x, mod, g, w_all, sgg, sgb, layer, tm, n_tok):
    bn, ln, d = x.shape
    nc = tm // LANES
    return pl.pallas_call(
        functools.partial(_inproj_kernel, n_tok=n_tok),
        grid=(bn, ln // tm),
        in_specs=[
            pl.BlockSpec((1, tm, d), lambda b, i: (b, i, 0)),
            pl.BlockSpec((1, 6, d), lambda b, i: (b, 0, 0)),
            _full((1, d)),
            _layer((d, N_IN), layer),
            _full((1, SG_W)), _full((1, SG_W)),
        ],
        out_specs=[
            pl.BlockSpec((1, tm, n_tok * ATT_W), lambda b, i: (b, i, 0)),
            pl.BlockSpec((1, nc, ATT_W, LANES), lambda b, i: (b, i, 0, 0)),
            pl.BlockSpec((1, tm, 2 * SG_W), lambda b, i: (b, i, 0)),
            pl.BlockSpec((1, tm, CV_W), lambda b, i: (b, i, 0)),
        ],
        out_shape=[
            jax.ShapeDtypeStruct((bn, ln, n_tok * ATT_W), BF16),
            jax.ShapeDtypeStruct((bn, ln // LANES, ATT_W, LANES), BF16),
            jax.ShapeDtypeStruct((bn, ln, 2 * SG_W), BF16),
            jax.ShapeDtypeStruct((bn, ln, CV_W), F32),
        ],
        compiler_params=_cparams(("parallel", "parallel")),
        name="inproj",
    )(x, mod, g, w_all, sgg, sgb)


GRID_ROWS = 2048 // GRID_W
RPB_W = 2 * NA_COLS - 1
RPB_PAD = 512
ROW_CFGS = ((0, 0), (2, 0), (8, 4), (GRID_ROWS - 4, GRID_ROWS - K_ROWS),
            (GRID_ROWS - 2, GRID_ROWS - K_ROWS))


def _table_kernel(rpb_ref, o_ref):
    r_base = pl.program_id(0) * NA_HEADS + pl.program_id(1) * 2
    shape = (GRID_W, LANES)
    kc = lax.broadcasted_iota(jnp.int32, shape, 0)
    lane = lax.broadcasted_iota(jnp.int32, shape, 1)
    low = lane < GRID_W
    qc = jnp.where(low, lane, lane - GRID_W)
    off = kc - qc + (NA_COLS - 1)
    sc = jnp.clip(qc - NA_COLS // 2, 0, GRID_W - NA_COLS)
    in_cols = (kc >= sc) & (kc < sc + NA_COLS)
    neg = jnp.full(shape, NEG_INF, F32)
    for hh in range(2):
        toep = []
        for u in range(2 * NA_ROWS - 1):
            acc = jnp.zeros(shape, F32)
            for w in range(RPB_W):
                acc = jnp.where(off == w, rpb_ref[r_base + hh, u * RPB_W + w], acc)
            toep.append(jnp.where(in_cols, acc * LOG2E, neg))
        for cfg, (r0, a) in enumerate(ROW_CFGS):
            for j in range(K_ROWS):
                kr = a + j
                halves = []
                for qr in (r0, r0 + 1):
                    sr = min(max(qr - NA_ROWS // 2, 0), GRID_ROWS - NA_ROWS)
                    inside = sr <= kr < sr + NA_ROWS
                    halves.append(toep[kr - qr + NA_ROWS - 1] if inside else neg)
                o_ref[0, cfg, 0, j * GRID_W:(j + 1) * GRID_W, hh * LANES:(hh + 1) * LANES] = (
                    jnp.where(low, halves[0], halves[1]))


def _bias_tables(na_rpb):
    assert Q_ROWS == 2
    flat = na_rpb.reshape(DEPTH * NA_HEADS, -1)
    flat = jnp.pad(flat, ((0, 0), (0, RPB_PAD - flat.shape[1])))
    nq2, nk = 2 * Q_ROWS * GRID_W, K_ROWS * GRID_W
    return pl.pallas_call(
        _table_kernel,
        grid=(DEPTH, HEAD_PAIRS),
        in_specs=[pl.BlockSpec(memory_space=pltpu.SMEM)],
        out_specs=pl.BlockSpec((1, len(ROW_CFGS), 1, nk, nq2), lambda l, p: (l, 0, p, 0, 0)),
        out_shape=jax.ShapeDtypeStruct((DEPTH, len(ROW_CFGS), HEAD_PAIRS, nk, nq2), F32),
        compiler_params=_cparams(("arbitrary", "arbitrary")),
        name="bias_tables",
    )(flat)


def _stack_heads(q2):
    lane = lax.broadcasted_iota(jnp.int32, q2.shape, 1)
    zero = jnp.zeros_like(q2)
    return jnp.concatenate([jnp.where(lane < HEAD_DIM, q2, zero),
                            jnp.where(lane < HEAD_DIM, zero, q2)], axis=0)


def _unstack_heads(o, n):
    lane = lax.broadcasted_iota(jnp.int32, (n, LANES), 1)
    return jnp.where(lane < HEAD_DIM, o[0:n], o[n:2 * n])


def _attn_kernel(q_ref, k_ref, vt_ref, kc_ref, vct_ref, tab_ref, o_ref):
    nq = Q_ROWS * GRID_W
    nk = K_ROWS * GRID_W
    nrb = q_ref.shape[1] // nq
    rows = q_ref.shape[1] // GRID_W
    row = lax.broadcasted_iota(jnp.int32, (LANES, nq), 0)
    ones_loc = jnp.ones((SUM_ROWS, nk), BF16)
    ones_ctx = jnp.ones((SUM_ROWS, kc_ref.shape[1]), BF16)

    def body(it, carry):
        def place(u):
            rb = it * ATT_UNROLL + u
            c0 = jnp.clip(rb - NA_ROWS // 4, 0, (rows - K_ROWS) // 2)
            cfg = jnp.where(rb < 2, rb, jnp.where(rb < nrb - 2, 2, rb - (nrb - 5)))
            return c0, cfg, pl.multiple_of(rb * nq, nq), pl.multiple_of(c0 * LANES, LANES)

        places = [place(u) for u in range(ATT_UNROLL)]

        def scores(u, p):
            _, cfg, q0, k0 = places[u]
            cols = slice(p * LANES, (p + 1) * LANES)
            qs = _stack_heads(q_ref[0, pl.ds(q0, nq), cols])
            s_loc = lax.dot_general(k_ref[0, pl.ds(k0, nk), cols], qs, _NT,
                                    preferred_element_type=F32) + tab_ref[0, cfg, p]
            s_ctx = lax.dot_general(kc_ref[0, :, cols], qs, _NT, preferred_element_type=F32)
            return s_loc, s_ctx

        steps = [(u, p) for u in range(ATT_UNROLL) for p in range(HEAD_PAIRS)]
        ahead = [scores(*st) for st in steps[:ATT_AHEAD]]
        for i, (u, p) in enumerate(steps):
            c0, _, q0, _ = places[u]
            cols = slice(p * LANES, (p + 1) * LANES)
            s_loc, s_ctx = ahead.pop(0)
            if i + ATT_AHEAD < len(steps):
                ahead.append(scores(*steps[i + ATT_AHEAD]))
            m = jnp.maximum(jnp.max(s_loc, axis=0, keepdims=True),
                            jnp.max(s_ctx, axis=0, keepdims=True))
            p_loc = jnp.exp2(s_loc - m).astype(BF16)
            p_ctx = jnp.exp2(s_ctx - m).astype(BF16)
            vt = jnp.concatenate([vt_ref[0, c0 + c, cols, :] for c in range(nk // LANES)]
                                 , axis=1)
            vct = jnp.concatenate([vct_ref[0, c, cols, :] for c in range(vct_ref.shape[1])],
                                  axis=1)
            o = (jnp.dot(jnp.concatenate([vt, ones_loc], axis=0), p_loc,
                         preferred_element_type=F32)
                 + jnp.dot(jnp.concatenate([vct, ones_ctx], axis=0), p_ctx,
                           preferred_element_type=F32))
            o = o[0:LANES] / o[LANES:LANES + 1]
            o2 = jnp.where(row < HEAD_DIM, o[:, 0:nq], o[:, nq:2 * nq])
            o_ref[0, pl.ds(q0, nq), cols] = o2.T.astype(BF16)
        return carry

    lax.fori_loop(0, nrb // ATT_UNROLL, body, 0)


def _attention(qk, vt, qkv_c, vt_c, tables, layer):
    bn, ln, _ = qk.shape
    lc = qkv_c.shape[1]
    return pl.pallas_call(
        _attn_kernel,
        grid=(bn,),
        in_specs=[
            pl.BlockSpec((1, ln, ATT_W), lambda b: (b, 0, 0)),
            pl.BlockSpec((1, ln, ATT_W), lambda b: (b, 0, 1)),
            pl.BlockSpec((1,) + vt.shape[1:], lambda b: (b, 0, 0, 0)),
            pl.BlockSpec((1, lc, ATT_W), lambda b: (b, 0, 1)),
            pl.BlockSpec((1,) + vt_c.shape[1:], lambda b: (b, 0, 0, 0)),
            pl.BlockSpec((1,) + tables.shape[1:], lambda b: (layer, 0, 0, 0, 0),
                         pipeline_mode=pl.Buffered(1)),
        ],
        out_specs=pl.BlockSpec((1, ln, ATT_W), lambda b: (b, 0, 0)),
        out_shape=jax.ShapeDtypeStruct((bn, ln, ATT_W), BF16),
        compiler_params=_cparams(("parallel",)),
        name="attention",
    )(qk, qk, vt, qkv_c, vt_c, tables)


def _ctx_attn_kernel(q_ref, k_ref, v_ref, o_ref):
    n = q_ref.shape[1]
    for p in range(HEAD_PAIRS):
        cols = slice(p * LANES, (p + 1) * LANES)
        qs = _stack_heads(q_ref[0, :, cols])
        s = lax.dot_general(qs, k_ref[0, :, cols], _NT, preferred_element_type=F32)
        e = jnp.exp2(s - jnp.max(s, axis=-1, keepdims=True))
        den = jnp.sum(e, axis=-1, keepdims=True)
        o = jnp.dot(e.astype(BF16), v_ref[0, :, cols], preferred_element_type=F32)
        o_ref[0, :, cols] = _unstack_heads(o / den, n).astype(BF16)


def _ctx_attention(qkv_c):
    bn, lc, _ = qkv_c.shape
    return pl.pallas_call(
        _ctx_attn_kernel,
        grid=(bn,),
        in_specs=[pl.BlockSpec((1, lc, ATT_W), lambda b, j=j: (b, 0, j)) for j in range(3)],
        out_specs=pl.BlockSpec((1, lc, ATT_W), lambda b: (b, 0, 0)),
        out_shape=jax.ShapeDtypeStruct((bn, lc, ATT_W), BF16),
        compiler_params=_cparams(("parallel",)),
        name="ctx_attention",
    )(qkv_c, qkv_c, qkv_c)


def _mix_kernel(att_ref, sg_ref, h_ref, hp_ref, hn_ref, x_ref, mod_ref, ws_ref,
                sbias_ref, cvw_ref, cvb_ref, cvg_ref, cvnb_ref, gmat_ref, wout_ref,
                o_ref, *, two_seq):
    tm = x_ref.shape[1]
    i = pl.program_id(1)
    nt = pl.num_programs(1)

    grp = lax.broadcasted_iota(jnp.int32, (SG_CHUNK, SG_W), 1) // (SG_W // SG_GROUPS)
    mixed = []
    for c in range(tm // SG_CHUNK):
        vch = sg_ref[0, c * SG_CHUNK:(c + 1) * SG_CHUNK, SG_W:2 * SG_W]
        vbd = jnp.concatenate([jnp.where(grp == g, vch, jnp.zeros_like(vch))
                               for g in range(SG_GROUPS)], axis=0)
        mixed.append(jnp.dot(ws_ref[...], vbd, preferred_element_type=F32) + sbias_ref[...])
    sg = sg_ref[0, :, 0:SG_W].astype(F32) * jnp.concatenate(mixed, axis=0)

    hf = jnp.concatenate([jnp.where(i > 0, hp_ref[0], 0.0), h_ref[0],
                          jnp.where(i < nt - 1, hn_ref[0], 0.0)], axis=0)
    th = tm // 2
    nh = th + 2 * CV_HALO
    zero_halo = jnp.zeros((CV_HALO, CV_W), F32)

    def group_mean(t):
        hi = t.astype(BF16)
        lo = (t - hi.astype(F32)).astype(BF16)
        return (jnp.dot(hi, gmat_ref[...], preferred_element_type=F32)
                + jnp.dot(lo, gmat_ref[...], preferred_element_type=F32))

    halves = [slice(0, th), slice(th, tm)]
    ys, ymean, out = [], [], []
    for rows in halves:
        r0 = rows.start
        if two_seq:
            hh = jnp.concatenate([zero_halo, h_ref[0, rows, :], zero_halo], axis=0)
        else:
            hh = hf[r0:r0 + nh]
        shifted = [hh] + [pltpu.roll(hh, nh - b, axis=0) for b in range(1, 8)]
        y = jnp.zeros((th, CV_W), F32) + cvb_ref[...]
        for k in range(CV_KERNEL):
            s = CV_HALO - CV_KERNEL // 2 + k
            y = y + cvw_ref[k:k + 1, :] * shifted[s % 8][8 * (s // 8):8 * (s // 8) + th]
        ys.append(y)
        ymean.append(group_mean(y))
        out.append(jnp.dot(att_ref[0, rows, :], wout_ref[0:ATT_W, :], preferred_element_type=F32)
                   + jnp.dot(sg[rows].astype(BF16), wout_ref[ATT_W:ATT_W + SG_W, :],
                             preferred_element_type=F32))
    yc = [y - m for y, m in zip(ys, ymean)]
    yvar = [group_mean(t * t) for t in yc]
    for rows, o, t, var in zip(halves, out, yc, yvar):
        yn = t * lax.rsqrt(var + EPS) * cvg_ref[...] + cvnb_ref[...]
        cv = yn * jax.nn.sigmoid(yn)
        o = o + jnp.dot(cv.astype(BF16), wout_ref[ATT_W + SG_W:, :], preferred_element_type=F32)
        o_ref[0, rows, :] = x_ref[0, rows, :] + mod_ref[0, 2:3, :] * o


def _mix(att, sg2, cvh, x, mod, ws, sbias, cvw, cvb, cvg, cvnb, gmat, wout_all, layer, tm,
         two_seq=False):
    bn, ln, d = x.shape
    nh = tm // CV_HALO
    last = ln // CV_HALO - 1
    return pl.pallas_call(
        functools.partial(_mix_kernel, two_seq=two_seq),
        grid=(bn, ln // tm),
        in_specs=[
            pl.BlockSpec((1, tm, ATT_W), lambda b, i: (b, i, 0)),
            pl.BlockSpec((1, tm, 2 * SG_W), lambda b, i: (b, i, 0)),
            pl.BlockSpec((1, tm, CV_W), lambda b, i: (b, i, 0)),
            pl.BlockSpec((1, CV_HALO, CV_W), lambda b, i: (b, jnp.maximum(i * nh - 1, 0), 0)),
            pl.BlockSpec((1, CV_HALO, CV_W), lambda b, i: (b, jnp.minimum((i + 1) * nh, last), 0)),
            pl.BlockSpec((1, tm, d), lambda b, i: (b, i, 0)),
            pl.BlockSpec((1, 6, d), lambda b, i: (b, 0, 0)),
            _full((SG_CHUNK, SG_GROUPS * SG_CHUNK)),
            _full((SG_CHUNK, SG_W)),
            _full((CV_KERNEL, CV_W)), _full((1, CV_W)), _full((1, CV_W)), _full((1, CV_W)),
            _full((CV_W, CV_W)),
            _layer((d, d), layer),
        ],
        out_specs=pl.BlockSpec((1, tm, d), lambda b, i: (b, i, 0)),
        out_shape=jax.ShapeDtypeStruct((bn, ln, d), F32),
        compiler_params=_cparams(("parallel", "parallel")),
        name="mix_out",
    )(att, sg2, cvh, cvh, cvh, x, mod, ws, sbias, cvw, cvb, cvg, cvnb, gmat, wout_all)


def _ffn_kernel(x_ref, xp_ref, xn_ref, mod_ref, g_ref, wup_ref, cw_ref, cb_ref, wdn_ref, fg_ref,
                o_ref, act_scr, *, final, two_seq):
    tm = x_ref.shape[1]
    i = pl.program_id(1)
    nt = pl.num_programs(1)
    shift, scale = mod_ref[0, 3:4, :], mod_ref[0, 4:5, :]
    xm = x_ref[0]
    hm = _norm_mod(xm, g_ref[...], shift, scale).astype(BF16)
    hp = jnp.where(i > 0, _norm_mod(xp_ref[0], g_ref[...], shift, scale), 0.0).astype(BF16)
    hn = jnp.where(i < nt - 1, _norm_mod(xn_ref[0], g_ref[...], shift, scale), 0.0).astype(BF16)
    hh = jnp.concatenate([hp, hm, hn], axis=0)

    n = tm + 2 * FFN_HALO
    row = lax.broadcasted_iota(jnp.int32, (tm, 1), 0)
    for j in range(D_FF // FF_CHUNK):
        cols = slice(j * FF_CHUNK, (j + 1) * FF_CHUNK)
        g = jnp.dot(hh, wup_ref[:, cols], preferred_element_type=F32)
        up = jnp.dot(hm, wup_ref[:, D_FF + j * FF_CHUNK:D_FF + (j + 1) * FF_CHUNK],
                     preferred_element_type=F32)
        g_prev = pltpu.roll(g, 1, axis=0)[FFN_HALO:FFN_HALO + tm]
        g_next = pltpu.roll(g, n - 1, axis=0)[FFN_HALO:FFN_HALO + tm]
        if two_seq:
            g_prev = jnp.where(row == tm // 2, 0.0, g_prev)
            g_next = jnp.where(row == tm // 2 - 1, 0.0, g_next)
        gc = (cb_ref[:, cols] + cw_ref[0:1, cols] * g_prev
              + cw_ref[1:2, cols] * g[FFN_HALO:FFN_HALO + tm] + cw_ref[2:3, cols] * g_next)
        act_scr[:, cols] = (gc * jax.nn.sigmoid(gc) * up).astype(BF16)
    acc = jnp.dot(act_scr[...], wdn_ref[...], preferred_element_type=F32)
    xo = xm + mod_ref[0, 5:6, :] * acc
    if final:
        ms = jnp.mean(xo * xo, axis=-1, keepdims=True)
        xo = xo * lax.rsqrt(ms + EPS) * fg_ref[...]
    o_ref[0] = xo


def _ffn(x, mod, g, wup_all, cw, cb, wdn_all, fg, layer, tm, final, two_seq=False):
    bn, ln, d = x.shape
    nh = tm // FFN_HALO
    last = ln // FFN_HALO - 1
    single = dict(pipeline_mode=pl.Buffered(1))
    return pl.pallas_call(
        functools.partial(_ffn_kernel, final=final, two_seq=two_seq),
        grid=(bn, ln // tm),
        in_specs=[
            pl.BlockSpec((1, tm, d), lambda b, i: (b, i, 0)),
            pl.BlockSpec((1, FFN_HALO, d), lambda b, i: (b, jnp.maximum(i * nh - 1, 0), 0)),
            pl.BlockSpec((1, FFN_HALO, d), lambda b, i: (b, jnp.minimum((i + 1) * nh, last), 0)),
            pl.BlockSpec((1, 6, d), lambda b, i: (b, 0, 0)),
            _full((1, d)),
            _layer((d, 2 * D_FF), layer, **single),
            _full((FFN_KERNEL, D_FF)), _full((1, D_FF)),
            _layer((D_FF, d), layer, **single),
            _full((1, d)),
        ],
        out_specs=pl.BlockSpec((1, tm, d), lambda b, i: (b, i, 0)),
        out_shape=jax.ShapeDtypeStruct((bn, ln, d), F32),
        scratch_shapes=[pltpu.VMEM((tm, D_FF), BF16)],
        compiler_params=_cparams(("parallel", "parallel")),
        name="conv_ffn",
    )(x, x, x, mod, g, wup_all, cw, cb, wdn_all, fg)


def kernel(x, c, ctx, c_ctx, ada_w, ada_b, norm1_g, w_in, na_rpb, sg_norm_g, sg_norm_b, sg_w,
           sg_b, cv_w, cv_b, cv_norm_g, cv_norm_b, w_out, norm2_g, ffn_w_up, ffn_conv_w,
           ffn_conv_b, ffn_w_down, final_norm_g):
    bn, ln, d = x.shape
    lc = ctx.shape[1]
    assert (d, ln, ln // GRID_W) == (D_MODEL, 2048, 32) and lc % SG_CHUNK == 0

    rows = -(-(bn + 1) // 8) * 8
    c_all = jnp.zeros((rows, d), F32).at[:bn].set(c).at[bn].set(c_ctx)
    mods = _ada_table(c_all, ada_w, ada_b)
    tables = _bias_tables(na_rpb)

    w_in_b = w_in.astype(BF16)
    w_out_b = w_out.astype(BF16)
    w_up_b = ffn_w_up.astype(BF16)
    w_dn_b = ffn_w_down.astype(BF16)
    ws_b = sg_w.transpose(0, 2, 1, 3).reshape(DEPTH, SG_CHUNK, SG_GROUPS * SG_CHUNK).astype(BF16)
    sbias = jnp.repeat(sg_b.transpose(0, 2, 1), SG_W // SG_GROUPS, axis=2)
    grp = np.arange(CV_W) // (CV_W // CV_GROUPS)
    gmat = jnp.asarray((grp[:, None] == grp[None, :]) / (CV_W // CV_GROUPS), BF16)
    fg = final_norm_g.reshape(1, d)

    tm = 512
    assert bn % 2 == 0 and 2 * lc == tm
    bp = bn // 2

    def pair(t):
        return t.reshape((bp, 2 * lc) + t.shape[2:])

    def unpair(t):
        return t.reshape((bn, lc) + t.shape[2:])

    xc = pair(ctx)
    for l in range(DEPTH):
        last = l == DEPTH - 1
        mod_l = mods[l, :bn].reshape(bn, 6, d)
        mod_c = jnp.broadcast_to(mods[l, bn].reshape(1, 6, d), (bp, 6, d))
        g1 = norm1_g[l].reshape(1, d)
        g2 = norm2_g[l].reshape(1, d)
        in_w = (w_in_b, sg_norm_g[l].reshape(1, SG_W), sg_norm_b[l].reshape(1, SG_W), l)
        mix_w = (ws_b[l], sbias[l], cv_w[l], cv_b[l].reshape(1, CV_W),
                 cv_norm_g[l].reshape(1, CV_W), cv_norm_b[l].reshape(1, CV_W), gmat, w_out_b, l)
        ffn_w = (w_up_b, ffn_conv_w[l], ffn_conv_b[l].reshape(1, D_FF), w_dn_b, fg, l)

        qkv_c, vt_c, sg_c, cvh_c = _inproj(xc, mod_c, g1, *in_w, tm, 3)
        qkv_c = unpair(qkv_c)
        vt_c = vt_c.reshape(bn, lc // LANES, ATT_W, LANES)
        qk, vt, sg2, cvh = _inproj(x, mod_l, g1, *in_w, 2 * tm, 2)
        att = _attention(qk, vt, qkv_c, vt_c, tables, l)
        x = _mix(att, sg2, cvh, x, mod_l, *mix_w, 2 * tm)
        x = _ffn(x, mod_l, g2, *ffn_w, 2 * tm, last)
        if not last:
            att_c = pair(_ctx_attention(qkv_c))
            xc = _mix(att_c, sg_c, cvh_c, xc, mod_c, *mix_w, tm, two_seq=True)
            xc = _ffn(xc, mod_c, g2, *ffn_w, tm, False, two_seq=True)
    return x
```
